```python
import math
import jax, jax.numpy as jnp
from jax import lax
import numpy as np

D_MODEL = 1024
BATCH = 8
SEQ = 2048
DEPTH = 4

N_MIXERS = 2
HEAD_DIM = 64
SWA_Q_HEADS = D_MODEL // HEAD_DIM
SWA_KV_HEADS = SWA_Q_HEADS // 4
SWA_GROUP = SWA_Q_HEADS // SWA_KV_HEADS
SWA_WINDOW = 128
FOX_HEADS = D_MODEL // HEAD_DIM
BLOCK = 128
ROPE_THETA = 500000.0
ROT_DIM = HEAD_DIM // 4
N_EXPERTS = 16
N_GROUPS = 4
EXPERTS_PER_GROUP = N_EXPERTS // N_GROUPS
TOP_K = 2
D_EXPERT = D_MODEL // 4
ALPHA = (2.0 * DEPTH) ** 0.25
BETA = (8.0 * DEPTH) ** -0.25
LN_EPS = 1e-5
NEG = -1e30
N_SWA_LAYERS = (DEPTH + 1) // 2
N_FOX_LAYERS = DEPTH // 2
SWA_IN = (SWA_Q_HEADS + 2 * SWA_KV_HEADS) * HEAD_DIM
FOX_IN = 3 * FOX_HEADS * HEAD_DIM + FOX_HEADS

kernel_name = "hybrid_swa_sink_fox_deepnorm_group_moe"


def layer_norm(x, gain, bias):
    xf = x.astype(jnp.float32)
    mu = jnp.mean(xf, axis=-1, keepdims=True)
    var = jnp.mean(jnp.square(xf - mu), axis=-1, keepdims=True)
    y = (xf - mu) * lax.rsqrt(var + LN_EPS)
    return (y * gain.astype(jnp.float32) + bias.astype(jnp.float32)).astype(x.dtype)


def partial_rope(x, pos):
    half = ROT_DIM // 2
    inv_freq = jnp.power(ROPE_THETA, -jnp.arange(half, dtype=jnp.float32) * (2.0 / ROT_DIM))
    ang = pos.astype(jnp.float32)[:, None] * inv_freq[None, :]
    cos = jnp.cos(ang)[None, :, None, :]
    sin = jnp.sin(ang)[None, :, None, :]
    xr = x[..., :ROT_DIM].astype(jnp.float32)
    x1, x2 = xr[..., :half], xr[..., half:]
    rot = jnp.concatenate([x1 * cos - x2 * sin, x2 * cos + x1 * sin], axis=-1).astype(x.dtype)
    return jnp.concatenate([rot, x[..., ROT_DIM:]], axis=-1)


def swa_sink_attention(x, w_in, sinks, w_o):
    B, S, _ = x.shape
    nb = S // BLOCK
    qd, kd = SWA_Q_HEADS * HEAD_DIM, SWA_KV_HEADS * HEAD_DIM
    proj = x @ w_in
    q = proj[..., :qd].reshape(B, S, SWA_Q_HEADS, HEAD_DIM)
    k = proj[..., qd:qd + kd].reshape(B, S, SWA_KV_HEADS, HEAD_DIM)
    v = proj[..., qd + kd:].reshape(B, S, SWA_KV_HEADS, HEAD_DIM)
    pos = jnp.arange(S)
    q = partial_rope(q, pos)
    k = partial_rope(k, pos)
    qb = q.reshape(B, nb, BLOCK, SWA_KV_HEADS, SWA_GROUP, HEAD_DIM)

    def with_prev(t):
        tb = t.reshape(B, nb, BLOCK, SWA_KV_HEADS, HEAD_DIM)
        prev = jnp.pad(tb, ((0, 0), (1, 0), (0, 0), (0, 0), (0, 0)))[:, :-1]
        return jnp.concatenate([prev, tb], axis=2)

    kk, vv = with_prev(k), with_prev(v)
    scores = jnp.einsum('bnqhgd,bnshd->bnhgqs', qb, kk).astype(jnp.float32) * (HEAD_DIM ** -0.5)
    q_pos = jnp.arange(nb)[:, None] * BLOCK + jnp.arange(BLOCK)[None, :]
    k_pos = jnp.arange(nb)[:, None] * BLOCK - BLOCK + jnp.arange(2 * BLOCK)[None, :]
    diff = q_pos[:, :, None] - k_pos[:, None, :]
    allowed = (diff >= 0) & (diff < SWA_WINDOW) & (k_pos[:, None, :] >= 0)
    scores = jnp.where(allowed[None, :, None, None], scores, NEG)
    sink = sinks.astype(jnp.float32).reshape(SWA_KV_HEADS, SWA_GROUP)
    sink_col = jnp.broadcast_to(sink[None, None, :, :, None, None], scores.shape[:-1] + (1,))
    probs = jax.nn.softmax(jnp.concatenate([scores, sink_col], axis=-1), axis=-1)[..., :-1]
    out = jnp.einsum('bnhgqs,bnshd->bnqhgd', probs.astype(vv.dtype), vv)
    return out.reshape(B, S, SWA_Q_HEADS * HEAD_DIM) @ w_o


def forgetting_attention(x, w_in, b_f, w_o):
    B, S, _ = x.shape
    hd_all = FOX_HEADS * HEAD_DIM
    proj = x @ w_in
    q = proj[..., :hd_all].reshape(B, S, FOX_HEADS, HEAD_DIM)
    k = proj[..., hd_all:2 * hd_all].reshape(B, S, FOX_HEADS, HEAD_DIM)
    v = proj[..., 2 * hd_all:3 * hd_all].reshape(B, S, FOX_HEADS, HEAD_DIM)
    f_logit = (proj[..., 3 * hd_all:] + b_f).astype(jnp.float32)
    c = jnp.cumsum(jax.nn.log_sigmoid(f_logit), axis=1).transpose(0, 2, 1)
    scale = HEAD_DIM ** -0.5
    outs = []
    for n in range(S // BLOCK):
        q0, end = n * BLOCK, (n + 1) * BLOCK
        s = jnp.einsum('bqhd,bkhd->bhqk', q[:, q0:end], k[:, :end]).astype(jnp.float32) * scale
        s = s + c[:, :, q0:end, None] - c[:, :, None, :end]
        causal = jnp.arange(q0, end)[:, None] >= jnp.arange(end)[None, :]
        p = jax.nn.softmax(jnp.where(causal, s, NEG), axis=-1)
        outs.append(jnp.einsum('bhqk,bkhd->bqhd', p.astype(v.dtype), v[:, :end]))
    out = jnp.concatenate(outs, axis=1).reshape(B, S, hd_all)
    return out @ w_o


def grouped_moe(h, w_router, b_router, w_gate, w_up, w_down):
    B, S, D = h.shape
    h2 = h.reshape(B * S, D)
    T = h2.shape[0]
    logits = (h2 @ w_router).astype(jnp.float32) + b_router.astype(jnp.float32)
    scores = jax.nn.softmax(logits, axis=-1)
    g = scores.reshape(T, N_GROUPS, EXPERTS_PER_GROUP)
    group_score = jnp.sum(lax.top_k(g, TOP_K)[0], axis=-1)
    best = jnp.argmax(group_score, axis=-1)
    in_group = jax.nn.one_hot(best, N_GROUPS, dtype=jnp.bool_)[:, :, None]
    masked = jnp.where(in_group, g, -1.0).reshape(T, N_EXPERTS)
    top_vals, top_idx = lax.top_k(masked, TOP_K)
    gates = top_vals / jnp.sum(top_vals, axis=-1, keepdims=True)
    gate_dense = jnp.sum(jax.nn.one_hot(top_idx, N_EXPERTS, dtype=jnp.float32) * gates[..., None], axis=1)
    hg = jnp.einsum('td,edf->tef', h2, w_gate)
    hu = jnp.einsum('td,edf->tef', h2, w_up)
    act = jax.nn.silu(hg) * hu * gate_dense[:, :, None].astype(h2.dtype)
    y = jnp.einsum('tef,efd->td', act, w_down)
    return y.reshape(B, S, D)


def setup_inputs(seed: int = 0) -> dict:
    key = jax.random.key(seed)
    ks = jax.random.split(key, 20)
    f32 = jnp.float32
    nrm = lambda k, shape, s: jax.random.normal(k, shape, f32) * s
    qk_cols_swa = (SWA_Q_HEADS + SWA_KV_HEADS) * HEAD_DIM
    v_cols_swa = SWA_KV_HEADS * HEAD_DIM
    swa_w_in = jnp.concatenate([
        nrm(ks[1], (N_SWA_LAYERS, D_MODEL, qk_cols_swa), D_MODEL ** -0.5),
        nrm(ks[2], (N_SWA_LAYERS, D_MODEL, v_cols_swa), D_MODEL ** -0.5 * BETA)], axis=-1)
    fox_hd = FOX_HEADS * HEAD_DIM
    fox_w_in = jnp.concatenate([
        nrm(ks[5], (N_FOX_LAYERS, D_MODEL, 2 * fox_hd), D_MODEL ** -0.5),
        nrm(ks[6], (N_FOX_LAYERS, D_MODEL, fox_hd), D_MODEL ** -0.5 * BETA),
        nrm(ks[7], (N_FOX_LAYERS, D_MODEL, FOX_HEADS), 0.5 * D_MODEL ** -0.5)], axis=-1)
    return {
        "x": jax.random.normal(ks[0], (BATCH, SEQ, D_MODEL), f32),
        "ln_gain": 1.0 + nrm(ks[3], (DEPTH, 2, D_MODEL), 0.02),
        "ln_bias": nrm(ks[4], (DEPTH, 2, D_MODEL), 0.02),
        "swa_w_in": swa_w_in,
        "swa_sinks": nrm(ks[8], (N_SWA_LAYERS, SWA_Q_HEADS), 0.5),
        "swa_w_o": nrm(ks[9], (N_SWA_LAYERS, SWA_Q_HEADS * HEAD_DIM, D_MODEL), (SWA_Q_HEADS * HEAD_DIM) ** -0.5 * BETA),
        "fox_w_in": fox_w_in,
        "fox_b_f": jax.random.uniform(ks[10], (N_FOX_LAYERS, FOX_HEADS), f32, 1.0, 6.0),
        "fox_w_o": nrm(ks[11], (N_FOX_LAYERS, fox_hd, D_MODEL), fox_hd ** -0.5 * BETA),
        "w_router": nrm(ks[12], (D_MODEL, N_EXPERTS), D_MODEL ** -0.5),
        "b_router": nrm(ks[13], (N_EXPERTS,), 0.01),
        "w_gate": nrm(ks[14], (DEPTH, N_EXPERTS, D_MODEL, D_EXPERT), D_MODEL ** -0.5),
        "w_up": nrm(ks[15], (DEPTH, N_EXPERTS, D_MODEL, D_EXPERT), D_MODEL ** -0.5),
        "w_down": nrm(ks[16], (DEPTH, N_EXPERTS, D_EXPERT, D_MODEL), D_EXPERT ** -0.5 * BETA),
    }


def reference(x, ln_gain, ln_bias, swa_w_in, swa_sinks, swa_w_o, fox_w_in, fox_b_f, fox_w_o,
              w_router, b_router, w_gate, w_up, w_down):
    for i in range(DEPTH):
        j = i // N_MIXERS
        if i % N_MIXERS == 0:
            mix = swa_sink_attention(x, swa_w_in[j], swa_sinks[j], swa_w_o[j])
        else:
            mix = forgetting_attention(x, fox_w_in[j], fox_b_f[j], fox_w_o[j])
        x = layer_norm(ALPHA * x + mix, ln_gain[i, 0], ln_bias[i, 0])
        ffn = grouped_moe(x, w_router, b_router, w_gate[i], w_up[i], w_down[i])
        x = layer_norm(ALPHA * x + ffn, ln_gain[i, 1], ln_bias[i, 1])
    return x
```

```python
import functools
import math

import jax
import jax.numpy as jnp
from jax import lax
from jax.experimental import pallas as pl
from jax.experimental.pallas import tpu as pltpu

F32 = jnp.float32
BF16 = jnp.bfloat16

HEAD_DIM = 64
ROT_DIM = 16
ROT_HALF = ROT_DIM // 2
ROPE_THETA = 500000.0
SWA_Q_HEADS = 16
SWA_KV_HEADS = 4
SWA_BLOCK = 128
FOX_HEADS = 16
N_EXPERTS = 16
N_GROUPS = 4
EXPERTS_PER_GROUP = 4
D_EXPERT = 256
DEPTH = 4
ALPHA = (2.0 * DEPTH) ** 0.25
LN_EPS = 1e-5
NEG = -1e30
QK_SCALE = HEAD_DIM ** -0.5

LANES = 128
VMEM_LIMIT = 56 * 1024 * 1024


def _cparams(*sem):
    return pltpu.CompilerParams(dimension_semantics=sem, vmem_limit_bytes=VMEM_LIMIT)


def _inproj_kernel(x_ref, w_ref, cos_ref, sa_ref, sb_ref, o_ref, *, tn, q_cols, rope_cols):
    j = pl.program_id(1)
    acc = jnp.dot(x_ref[...].astype(BF16), w_ref[...], preferred_element_type=F32)
    col = j * tn + lax.broadcasted_iota(jnp.int32, (1, tn), 1)
    if rope_cols:
        reps = tn // LANES
        in_rope = col < rope_cols
        cosf = jnp.where(in_rope, jnp.tile(cos_ref[...], (1, reps)), 1.0)
        saf = jnp.where(in_rope, jnp.tile(sa_ref[...], (1, reps)), 0.0)
        sbf = jnp.where(in_rope, jnp.tile(sb_ref[...], (1, reps)), 0.0)
        acc = (acc * cosf + pltpu.roll(acc, ROT_HALF, 1) * saf
               + pltpu.roll(acc, tn - ROT_HALF, 1) * sbf)
    acc = acc * jnp.where(col < q_cols, QK_SCALE, 1.0)
    o_ref[...] = acc.astype(o_ref.dtype)


def _rope_tables(seq):
    inv_freq = jnp.power(ROPE_THETA, -jnp.arange(ROT_HALF, dtype=F32) * (2.0 / ROT_DIM))
    ang = jnp.arange(seq, dtype=F32)[:, None] * inv_freq[None, :]
    cos, sin = jnp.cos(ang), jnp.sin(ang)
    ones = jnp.ones((seq, HEAD_DIM - ROT_DIM), F32)
    zeros = jnp.zeros((seq, HEAD_DIM - ROT_DIM), F32)
    z8 = jnp.zeros((seq, ROT_HALF), F32)
    cos_h = jnp.concatenate([cos, cos, ones], axis=1)
    sa_h = jnp.concatenate([z8, sin, zeros], axis=1)
    sb_h = jnp.concatenate([-sin, z8, zeros], axis=1)
    rep = LANES // HEAD_DIM
    return tuple(jnp.tile(t, (1, rep)) for t in (cos_h, sa_h, sb_h))


def _in_projection(x2, w, seq, *, q_cols, rope_cols, tables):
    t_tokens, d = x2.shape
    n = w.shape[1]
    tm = min(1024, seq)
    tn = 512
    assert t_tokens % tm == 0 and seq % tm == 0 and n % tn == 0
    pos_blocks = seq // tm
    kern = functools.partial(_inproj_kernel, tn=tn, q_cols=q_cols, rope_cols=rope_cols)
    tab_spec = pl.BlockSpec((tm, LANES), lambda i, j: (i % pos_blocks, 0))
    return pl.pallas_call(
        kern,
        out_shape=jax.ShapeDtypeStruct((t_tokens, n), BF16),
        grid=(t_tokens // tm, n // tn),
        in_specs=[pl.BlockSpec((tm, d), lambda i, j: (i, 0)),
                  pl.BlockSpec((d, tn), lambda i, j: (0, j)),
                  tab_spec, tab_spec, tab_spec],
        out_specs=pl.BlockSpec((tm, tn), lambda i, j: (i, j)),
        compiler_params=_cparams("parallel", "arbitrary"),
        name="in_projection",
    )(x2, w, *tables)


def _swa_kernel(sink_ref, q_ref, kp_ref, kc_ref, vp_ref, vc_ref, o_ref):
    n = pl.program_id(1)
    blk = SWA_BLOCK
    lane = lax.broadcasted_iota(jnp.int32, (1, LANES), 1)
    lo = lane < HEAD_DIM
    r = lax.broadcasted_iota(jnp.int32, (blk, 2 * blk), 0)
    c = lax.broadcasted_iota(jnp.int32, (blk, 2 * blk), 1)
    allowed = (c > r) & (c <= r + blk) & ((c >= blk) | (n > 0))
    k_all = jnp.concatenate([kp_ref[0], kc_ref[0]], axis=0).astype(F32)
    v_all = jnp.concatenate([vp_ref[0], vc_ref[0]], axis=0).astype(F32)
    for g in range(SWA_KV_HEADS):
        grp = g // 2
        kf = k_all[:, grp * LANES:(grp + 1) * LANES]
        vf = v_all[:, grp * LANES:(grp + 1) * LANES]
        kr = pltpu.roll(kf, HEAD_DIM, 1)
        vr = pltpu.roll(vf, HEAD_DIM, 1)
        own_lo = (g % 2 == 0)
        k_src_lo, k_src_hi = (kf, kr) if own_lo else (kr, kf)
        k_lo = jnp.where(lo, k_src_lo, 0.0).astype(BF16)
        k_hi = jnp.where(lo, 0.0, k_src_hi).astype(BF16)
        v_dup = jnp.where(lo, vf if own_lo else vr, vr if own_lo else vf).astype(BF16)
        for pp in range(2):
            p = 2 * g + pp
            qp = q_ref[0, :, p * LANES:(p + 1) * LANES]
            outs = []
            for e, kk in enumerate((k_lo, k_hi)):
                s = lax.dot_general(qp, kk, (((1,), (1,)), ((), ())),
                                    preferred_element_type=F32)
                s = jnp.where(allowed, s, NEG)
                sink = sink_ref[2 * p + e]
                m = jnp.maximum(jnp.max(s, axis=1, keepdims=True), sink)
                pr = jnp.exp(s - m)
                den = jnp.sum(pr, axis=1, keepdims=True) + jnp.exp(sink - m)
                o = jnp.dot(pr.astype(BF16), v_dup, preferred_element_type=F32)
                outs.append(o / den)
            o_ref[0, :, p * LANES:(p + 1) * LANES] = jnp.where(lo, outs[0], outs[1]).astype(o_ref.dtype)


def _swa_attention(proj, sinks, batch, seq):
    blk = SWA_BLOCK
    nb = seq // blk
    qd = SWA_Q_HEADS * HEAD_DIM
    kd = SWA_KV_HEADS * HEAD_DIM
    kcol, vcol = qd // kd, qd // kd + 1
    prev = lambda b, n: jnp.maximum(n - 1, 0)
    return pl.pallas_call(
        _swa_kernel,
        out_shape=jax.ShapeDtypeStruct((batch, seq, qd), BF16),
        grid=(batch, nb),
        in_specs=[pl.BlockSpec(memory_space=pltpu.SMEM),
                  pl.BlockSpec((1, blk, qd), lambda b, n: (b, n, 0)),
                  pl.BlockSpec((1, blk, kd), lambda b, n: (b, prev(b, n), kcol)),
                  pl.BlockSpec((1, blk, kd), lambda b, n: (b, n, kcol)),
                  pl.BlockSpec((1, blk, kd), lambda b, n: (b, prev(b, n), vcol)),
                  pl.BlockSpec((1, blk, kd), lambda b, n: (b, n, vcol))],
        out_specs=pl.BlockSpec((1, blk, qd), lambda b, n: (b, n, 0)),
        compiler_params=_cparams("parallel", "arbitrary"),
        name="swa_attention",
    )(sinks, proj, proj, proj, proj, proj)


def _decay_kernel(x_ref, wf_ref, bf_ref, ccol_ref, crow_ref):
    seq = x_ref.shape[1]
    f = jnp.dot(x_ref[0].astype(BF16), wf_ref[...], preferred_element_type=F32) + bf_ref[...]
    ls = jnp.minimum(f, 0.0) - jnp.log1p(jnp.exp(-jnp.abs(f)))
    row = lax.broadcasted_iota(jnp.int32, ls.shape, 0)
    c = ls
    k = 1
    while k < seq:
        c = c + jnp.where(row >= k, pltpu.roll(c, k, 0), 0.0)
        k *= 2
    ccol_ref[0] = c
    crow_ref[0] = c.T


def _decay_cumsum(x3, wf, bfv):
    batch, seq, d = x3.shape
    return pl.pallas_call(
        _decay_kernel,
        out_shape=(jax.ShapeDtypeStruct((batch, seq, LANES), F32),
                   jax.ShapeDtypeStruct((batch, LANES, seq), F32)),
        grid=(batch,),
        in_specs=[pl.BlockSpec((1, seq, d), lambda b: (b, 0, 0)),
                  pl.BlockSpec((d, LANES), lambda b: (0, 0)),
                  pl.BlockSpec((1, LANES), lambda b: (0, 0))],
        out_specs=(pl.BlockSpec((1, seq, LANES), lambda b: (b, 0, 0)),
                   pl.BlockSpec((1, LANES, seq), lambda b: (b, 0, 0))),
        compiler_params=_cparams("parallel"),
        name="fox_decay_cumsum",
    )(x3, wf, bfv)


def _fox_kernel(q_ref, k_ref, v_ref, ccol_ref, crow_ref, o_ref, *, tq):
    hp = pl.program_id(1)
    qi = pl.program_id(2)
    lane = lax.broadcasted_iota(jnp.int32, (1, LANES), 1)
    lo = lane < HEAD_DIM
    q = q_ref[0]
    zero = jnp.zeros_like(q)
    q_heads = (jnp.where(lo, q, zero), jnp.where(lo, zero, q))
    ccol = ccol_ref[0]
    lane_q = lax.broadcasted_iota(jnp.int32, ccol.shape, 1)
    ct = [jnp.sum(jnp.where(lane_q == 2 * hp + e, ccol, 0.0), axis=1, keepdims=True)
          for e in range(2)]
    rr = lax.broadcasted_iota(jnp.int32, (tq, tq), 0)
    cc = lax.broadcasted_iota(jnp.int32, (tq, tq), 1)
    causal = rr >= cc

    def block(j, carry, masked):
        start = pl.multiple_of(j * tq, tq)
        kj = k_ref[0, pl.ds(start, tq), :]
        vj = v_ref[0, pl.ds(start, tq), :]
        new = []
        for e in range(2):
            m, l, acc = carry[e]
            cs = crow_ref[0, pl.ds(2 * hp + e, 1), pl.ds(start, tq)]
            s = lax.dot_general(q_heads[e], kj, (((1,), (1,)), ((), ())),
                                preferred_element_type=F32)
            s = s + (ct[e] - cs)
            if masked:
                s = jnp.where(causal, s, NEG)
            m_new = jnp.maximum(m, jnp.max(s, axis=1, keepdims=True))
            p = jnp.exp(s - m_new)
            a = jnp.exp(m - m_new)
            l = a * l + jnp.sum(p, axis=1, keepdims=True)
            acc = a * acc + jnp.dot(p.astype(BF16), vj, preferred_element_type=F32)
            new.append((m_new, l, acc))
        return tuple(new)

    init = tuple((jnp.full((tq, 1), NEG, F32), jnp.zeros((tq, 1), F32),
                  jnp.zeros((tq, LANES), F32)) for _ in range(2))
    carry = lax.fori_loop(0, qi, lambda j, cr: block(j, cr, False), init)
    carry = block(qi, carry, True)
    o0 = carry[0][2] / carry[0][1]
    o1 = carry[1][2] / carry[1][1]
    o_ref[0] = jnp.where(lo, o0, o1).astype(o_ref.dtype)


def _fox_attention(proj, ccol, crow, batch, seq):
    tq = min(256, seq)
    hd_all = FOX_HEADS * HEAD_DIM
    pairs = hd_all // LANES
    kern = functools.partial(_fox_kernel, tq=tq)
    return pl.pallas_call(
        kern,
        out_shape=jax.ShapeDtypeStruct((batch, seq, hd_all), BF16),
        grid=(batch, pairs, seq // tq),
        in_specs=[pl.BlockSpec((1, tq, LANES), lambda b, h, i: (b, i, h)),
                  pl.BlockSpec((1, seq, LANES), lambda b, h, i: (b, 0, pairs + h)),
                  pl.BlockSpec((1, seq, LANES), lambda b, h, i: (b, 0, 2 * pairs + h)),
                  pl.BlockSpec((1, tq, LANES), lambda b, h, i: (b, i, 0)),
                  pl.BlockSpec((1, LANES, seq), lambda b, h, i: (b, 0, 0))],
        out_specs=pl.BlockSpec((1, tq, LANES), lambda b, h, i: (b, i, h)),
        compiler_params=_cparams("parallel", "parallel", "arbitrary"),
        name="fox_attention",
    )(proj, proj, proj, ccol, crow)


def _layer_norm(y, gain, bias):
    mu = jnp.mean(y, axis=-1, keepdims=True)
    d = y - mu
    var = jnp.mean(d * d, axis=-1, keepdims=True)
    return d * lax.rsqrt(var + LN_EPS) * gain + bias


def _split_bf16(v):
    hi = v.astype(BF16)
    lo = (v - hi.astype(F32)).astype(BF16)
    return hi, lo


def _route(logits_t):
    rows = [logits_t[e:e + 1, :] for e in range(N_EXPERTS)]
    mx = functools.reduce(jnp.maximum, rows)
    ex = [jnp.exp(v - mx) for v in rows]
    den = functools.reduce(lambda a, b: a + b, ex)
    sc = [v / den for v in ex]
    gscore = []
    for g in range(N_GROUPS):
        mem = sc[g * EXPERTS_PER_GROUP:(g + 1) * EXPERTS_PER_GROUP]
        pairs = [mem[i] + mem[j] for i in range(4) for j in range(i + 1, 4)]
        gscore.append(functools.reduce(jnp.maximum, pairs))
    gmax = functools.reduce(jnp.maximum, gscore)
    taken = None
    gates = []
    for g in range(N_GROUPS):
        eq = gscore[g] == gmax
        best = eq if taken is None else eq & jnp.logical_not(taken)
        taken = eq if taken is None else taken | eq
        mem = sc[g * EXPERTS_PER_GROUP:(g + 1) * EXPERTS_PER_GROUP]
        sel = []
        for i in range(4):
            rank = jnp.zeros_like(mem[i])
            for j in range(4):
                if j == i:
                    continue
                ahead = (mem[j] > mem[i]) | ((mem[j] == mem[i]) & (j < i))
                rank = rank + jnp.where(ahead, 1.0, 0.0)
            sel.append(best & (rank < 2.0))
        tot = functools.reduce(lambda a, b: a + b,
                               [jnp.where(sel[i], mem[i], 0.0) for i in range(4)])
        for i in range(4):
            gates.append(jnp.where(sel[i], mem[i] / tot, 0.0))
    return jnp.concatenate(gates, axis=0)


def _outproj_kernel(a_ref, wo_ref, x_ref, g_ref, b_ref, wrh_ref, wrl_ref, br_ref,
                    x1_ref, gate_ref):
    mix = jnp.dot(a_ref[...], wo_ref[...], preferred_element_type=F32)
    x1 = _layer_norm(ALPHA * x_ref[...] + mix, g_ref[...], b_ref[...])
    x1_ref[...] = x1
    hi, lo = _split_bf16(x1)
    logits = (jnp.dot(hi, wrh_ref[...], preferred_element_type=F32)
              + (jnp.dot(lo, wrh_ref[...], preferred_element_type=F32)
                 + jnp.dot(hi, wrl_ref[...], preferred_element_type=F32))) + br_ref[...]
    gates_t = _route(logits.T[:N_EXPERTS, :])
    pad = jnp.zeros((LANES - N_EXPERTS, gates_t.shape[1]), F32)
    gate_ref[...] = jnp.concatenate([gates_t, pad], axis=0).T


def _outproj_ln_router(a2, wo, x2, gain, bias, wr_hi, wr_lo, br):
    t_tokens, d = x2.shape
    tm = min(512, t_tokens)
    row = lambda i: (i, 0)
    fixed = lambda i: (0, 0)
    return pl.pallas_call(
        _outproj_kernel,
        out_shape=(jax.ShapeDtypeStruct((t_tokens, d), F32),
                   jax.ShapeDtypeStruct((t_tokens, LANES), F32)),
        grid=(t_tokens // tm,),
        in_specs=[pl.BlockSpec((tm, d), row),
                  pl.BlockSpec((d, d), fixed),
                  pl.BlockSpec((tm, d), row),
                  pl.BlockSpec((1, d), fixed),
                  pl.BlockSpec((1, d), fixed),
                  pl.BlockSpec((d, LANES), fixed),
                  pl.BlockSpec((d, LANES), fixed),
                  pl.BlockSpec((1, LANES), fixed)],
        out_specs=(pl.BlockSpec((tm, d), row), pl.BlockSpec((tm, LANES), row)),
        compiler_params=_cparams("parallel"),
        name="outproj_ln_router",
    )(a2, wo, x2, gain, bias, wr_hi, wr_lo, br)


def _moe_kernel(x_ref, gate_ref, wgu_ref, wd_ref, g_ref, b_ref, o_ref, xb_ref, acc_ref):
    e = pl.program_id(1)

    @pl.when(e == 0)
    def _():
        xb_ref[...] = x_ref[...].astype(BF16)
        acc_ref[...] = jnp.zeros_like(acc_ref)

    h = jnp.dot(xb_ref[...], wgu_ref[0], preferred_element_type=F32)
    hg, hu = h[:, :D_EXPERT], h[:, D_EXPERT:]
    gates = gate_ref[...]
    lane = lax.broadcasted_iota(jnp.int32, gates.shape, 1)
    ge = jnp.sum(jnp.where(lane == e, gates, 0.0), axis=1, keepdims=True)
    act = hg * jax.nn.sigmoid(hg) * hu * ge
    acc_ref[...] += jnp.dot(act.astype(BF16), wd_ref[0], preferred_element_type=F32)

    @pl.when(e == N_EXPERTS - 1)
    def _():
        o_ref[...] = _layer_norm(ALPHA * x_ref[...] + acc_ref[...], g_ref[...], b_ref[...])


def _moe_ln(x2, gates, wgu, wd, gain, bias):
    t_tokens, d = x2.shape
    tm = min(512, t_tokens)
    row = lambda i, e: (i, 0)
    fixed = lambda i, e: (0, 0)
    return pl.pallas_call(
        _moe_kernel,
        out_shape=jax.ShapeDtypeStruct((t_tokens, d), F32),
        grid=(t_tokens // tm, N_EXPERTS),
        in_specs=[pl.BlockSpec((tm, d), row),
                  pl.BlockSpec((tm, LANES), row),
                  pl.BlockSpec((1, d, 2 * D_EXPERT), lambda i, e: (e, 0, 0)),
                  pl.BlockSpec((1, D_EXPERT, d), lambda i, e: (e, 0, 0)),
                  pl.BlockSpec((1, d), fixed),
                  pl.BlockSpec((1, d), fixed)],
        out_specs=pl.BlockSpec((tm, d), row),
        scratch_shapes=[pltpu.VMEM((tm, d), BF16), pltpu.VMEM((tm, d), F32)],
        compiler_params=_cparams("parallel", "arbitrary"),
        name="moe_ln",
    )(x2, gates, wgu, wd, gain, bias)


def _pad_lanes(w):
    return jnp.pad(w, ((0, 0), (0, LANES - w.shape[1])))


def kernel(x, ln_gain, ln_bias, swa_w_in, swa_sinks, swa_w_o, fox_w_in, fox_b_f, fox_w_o,
           w_router, b_router, w_gate, w_up, w_down):
    batch, seq, d = x.shape
    t_tokens = batch * seq
    tables = _rope_tables(seq)
    wr_hi, wr_lo = _split_bf16(_pad_lanes(w_router))
    br = _pad_lanes(b_router[None, :])
    fox_qkv = 3 * FOX_HEADS * HEAD_DIM
    x2 = x.reshape(t_tokens, d)
    for i in range(DEPTH):
        j = i // 2
        if i % 2 == 0:
            proj = _in_projection(x2, swa_w_in[j].astype(BF16), seq,
                                  q_cols=SWA_Q_HEADS * HEAD_DIM,
                                  rope_cols=(SWA_Q_HEADS + SWA_KV_HEADS) * HEAD_DIM,
                                  tables=tables)
            attn = _swa_attention(proj.reshape(batch, seq, -1), swa_sinks[j], batch, seq)
            w_o = swa_w_o[j]
        else:
            w_in = fox_w_in[j]
            proj = _in_projection(x2, w_in[:, :fox_qkv].astype(BF16), seq,
                                  q_cols=FOX_HEADS * HEAD_DIM, rope_cols=0, tables=tables)
            ccol, crow = _decay_cumsum(x2.reshape(batch, seq, d),
                                       _pad_lanes(w_in[:, fox_qkv:]).astype(BF16),
                                       _pad_lanes(fox_b_f[j][None, :]))
            attn = _fox_attention(proj.reshape(batch, seq, -1), ccol, crow, batch, seq)
            w_o = fox_w_o[j]
        x1, gates = _outproj_ln_router(attn.reshape(t_tokens, d), w_o.astype(BF16), x2,
                                       ln_gain[i, 0][None, :], ln_bias[i, 0][None, :],
                                       wr_hi, wr_lo, br)
        wgu = jnp.concatenate([w_gate[i], w_up[i]], axis=-1).astype(BF16)
        x2 = _moe_ln(x1, gates, wgu, w_down[i].astype(BF16),
                     ln_gain[i, 1][None, :], ln_bias[i, 1][None, :])
    return x2.reshape(batch, seq, d)
```

```python
import functools
import math

import jax
import jax.numpy as jnp
from jax import lax
from jax.experimental import pallas as pl
from jax.experimental.pallas import tpu as pltpu

F32 = jnp.float32
BF16 = jnp.bfloat16

HEAD_DIM = 64
ROT_DIM = 16
ROT_HALF = ROT_DIM // 2
ROPE_THETA = 500000.0
SWA_Q_HEADS = 16
SWA_KV_HEADS = 4
SWA_BLOCK = 128
FOX_HEADS = 16
N_EXPERTS = 16
N_GROUPS = 4
EXPERTS_PER_GROUP = 4
D_EXPERT = 256
DEPTH = 4
ALPHA = (2.0 * DEPTH) ** 0.25
LN_EPS = 1e-5
NEG = -1e30
QK_SCALE = HEAD_DIM ** -0.5
LOG2E = math.log2(math.e)

LANES = 128
VMEM_LIMIT = 56 * 1024 * 1024


def _cparams(*sem):
    return pltpu.CompilerParams(dimension_semantics=sem, vmem_limit_bytes=VMEM_LIMIT)


def _inproj_kernel(x_ref, w_ref, cos_ref, sa_ref, sb_ref, o_ref, *, tn, q_cols, q_scale,
                   rope_cols):
    j = pl.program_id(1)
    acc = jnp.dot(x_ref[...].astype(BF16), w_ref[...], preferred_element_type=F32)
    col = j * tn + lax.broadcasted_iota(jnp.int32, (1, tn), 1)
    if rope_cols:
        reps = tn // LANES
        in_rope = col < rope_cols
        cosf = jnp.where(in_rope, jnp.tile(cos_ref[...], (1, reps)), 1.0)
        saf = jnp.where(in_rope, jnp.tile(sa_ref[...], (1, reps)), 0.0)
        sbf = jnp.where(in_rope, jnp.tile(sb_ref[...], (1, reps)), 0.0)
        acc = (acc * cosf + pltpu.roll(acc, ROT_HALF, 1) * saf
               + pltpu.roll(acc, tn - ROT_HALF, 1) * sbf)
    acc = acc * jnp.where(col < q_cols, q_scale, 1.0)
    o_ref[...] = acc.astype(o_ref.dtype)


def _rope_tables(seq):
    inv_freq = jnp.power(ROPE_THETA, -jnp.arange(ROT_HALF, dtype=F32) * (2.0 / ROT_DIM))
    ang = jnp.arange(seq, dtype=F32)[:, None] * inv_freq[None, :]
    cos, sin = jnp.cos(ang), jnp.sin(ang)
    ones = jnp.ones((seq, HEAD_DIM - ROT_DIM), F32)
    zeros = jnp.zeros((seq, HEAD_DIM - ROT_DIM), F32)
    z8 = jnp.zeros((seq, ROT_HALF), F32)
    cos_h = jnp.concatenate([cos, cos, ones], axis=1)
    sa_h = jnp.concatenate([z8, sin, zeros], axis=1)
    sb_h = jnp.concatenate([-sin, z8, zeros], axis=1)
    rep = LANES // HEAD_DIM
    return tuple(jnp.tile(t, (1, rep)) for t in (cos_h, sa_h, sb_h))


def _in_projection(x2, w, seq, *, q_cols, q_scale, rope_cols, tables):
    t_tokens, d = x2.shape
    n = w.shape[1]
    tm = min(1024, seq)
    tn = 512
    assert t_tokens % tm == 0 and seq % tm == 0 and n % tn == 0
    pos_blocks = seq // tm
    kern = functools.partial(_inproj_kernel, tn=tn, q_cols=q_cols, q_scale=q_scale,
                             rope_cols=rope_cols)
    tab_spec = pl.BlockSpec((tm, LANES), lambda i, j: (i % pos_blocks, 0))
    return pl.pallas_call(
        kern,
        out_shape=jax.ShapeDtypeStruct((t_tokens, n), BF16),
        grid=(t_tokens // tm, n // tn),
        in_specs=[pl.BlockSpec((tm, d), lambda i, j: (i, 0)),
                  pl.BlockSpec((d, tn), lambda i, j: (0, j)),
                  tab_spec, tab_spec, tab_spec],
        out_specs=pl.BlockSpec((tm, tn), lambda i, j: (i, j)),
        compiler_params=_cparams("parallel", "arbitrary"),
        name="in_projection",
    )(x2, w, *tables)


def _swa_kernel(sink_ref, q_ref, kp_ref, kc_ref, vp_ref, vc_ref, o_ref):
    n = pl.program_id(1)
    blk = SWA_BLOCK
    lane = lax.broadcasted_iota(jnp.int32, (1, LANES), 1)
    lo = lane < HEAD_DIM
    r = lax.broadcasted_iota(jnp.int32, (blk, 2 * blk), 0)
    c = lax.broadcasted_iota(jnp.int32, (blk, 2 * blk), 1)
    allowed = (c > r) & (c <= r + blk) & ((c >= blk) | (n > 0))
    k_all = jnp.concatenate([kp_ref[0], kc_ref[0]], axis=0).astype(F32)
    v_all = jnp.concatenate([vp_ref[0], vc_ref[0]], axis=0).astype(F32)
    for g in range(SWA_KV_HEADS):
        grp = g // 2
        kf = k_all[:, grp * LANES:(grp + 1) * LANES]
        vf = v_all[:, grp * LANES:(grp + 1) * LANES]
        kr = pltpu.roll(kf, HEAD_DIM, 1)
        vr = pltpu.roll(vf, HEAD_DIM, 1)
        own_lo = (g % 2 == 0)
        k_src_lo, k_src_hi = (kf, kr) if own_lo else (kr, kf)
        k_lo = jnp.where(lo, k_src_lo, 0.0).astype(BF16)
        k_hi = jnp.where(lo, 0.0, k_src_hi).astype(BF16)
        v_dup = jnp.where(lo, vf if own_lo else vr, vr if own_lo else vf).astype(BF16)
        for pp in range(2):
            p = 2 * g + pp
            qp = q_ref[0, :, p * LANES:(p + 1) * LANES]
            outs = []
            for e, kk in enumerate((k_lo, k_hi)):
                s = lax.dot_general(qp, kk, (((1,), (1,)), ((), ())),
                                    preferred_element_type=F32)
                s = jnp.where(allowed, s, NEG)
                sink = sink_ref[2 * p + e]
                m = jnp.maximum(jnp.max(s, axis=1, keepdims=True), sink)
                pr = jnp.exp(s - m)
                den = jnp.sum(pr, axis=1, keepdims=True) + jnp.exp(sink - m)
                o = jnp.dot(pr.astype(BF16), v_dup, preferred_element_type=F32)
                outs.append(o / den)
            o_ref[0, :, p * LANES:(p + 1) * LANES] = jnp.where(lo, outs[0], outs[1]).astype(o_ref.dtype)


def _swa_attention(proj, sinks, batch, seq):
    blk = SWA_BLOCK
    nb = seq // blk
    qd = SWA_Q_HEADS * HEAD_DIM
    kd = SWA_KV_HEADS * HEAD_DIM
    kcol, vcol = qd // kd, qd // kd + 1
    prev = lambda b, n: jnp.maximum(n - 1, 0)
    return pl.pallas_call(
        _swa_kernel,
        out_shape=jax.ShapeDtypeStruct((batch, seq, qd), BF16),
        grid=(batch, nb),
        in_specs=[pl.BlockSpec(memory_space=pltpu.SMEM),
                  pl.BlockSpec((1, blk, qd), lambda b, n: (b, n, 0)),
                  pl.BlockSpec((1, blk, kd), lambda b, n: (b, prev(b, n), kcol)),
                  pl.BlockSpec((1, blk, kd), lambda b, n: (b, n, kcol)),
                  pl.BlockSpec((1, blk, kd), lambda b, n: (b, prev(b, n), vcol)),
                  pl.BlockSpec((1, blk, kd), lambda b, n: (b, n, vcol))],
        out_specs=pl.BlockSpec((1, blk, qd), lambda b, n: (b, n, 0)),
        compiler_params=_cparams("parallel", "arbitrary"),
        name="swa_attention",
    )(sinks, proj, proj, proj, proj, proj)


def _decay_kernel(x_ref, wf_ref, bf_ref, ccol_ref):
    seq = x_ref.shape[1]
    f = jnp.dot(x_ref[0].astype(BF16), wf_ref[...], preferred_element_type=F32) + bf_ref[...]
    ls = jnp.minimum(f, 0.0) - jnp.log1p(jnp.exp(-jnp.abs(f)))
    row = lax.broadcasted_iota(jnp.int32, ls.shape, 0)
    c = ls
    k = 1
    while k < seq:
        c = c + jnp.where(row >= k, pltpu.roll(c, k, 0), 0.0)
        k *= 2
    ccol_ref[0] = c * LOG2E


def _decay_cumsum(x3, wf, bfv):
    batch, seq, d = x3.shape
    return pl.pallas_call(
        _decay_kernel,
        out_shape=jax.ShapeDtypeStruct((batch, seq, LANES), F32),
        grid=(batch,),
        in_specs=[pl.BlockSpec((1, seq, d), lambda b: (b, 0, 0)),
                  pl.BlockSpec((d, LANES), lambda b: (0, 0)),
                  pl.BlockSpec((1, LANES), lambda b: (0, 0))],
        out_specs=pl.BlockSpec((1, seq, LANES), lambda b: (b, 0, 0)),
        compiler_params=_cparams("parallel"),
        name="fox_decay_cumsum",
    )(x3, wf, bfv)


def _split3_bf16(v):
    hi = v.astype(BF16).astype(F32)
    r = v - hi
    mid = r.astype(BF16).astype(F32)
    lo = (r - mid).astype(BF16).astype(F32)
    return hi, mid, lo


def _fox_kernel(q_ref, k_ref, v_ref, ccol_ref, o_ref, kaug_ref, *, tq, nq, pps):
    hp = pl.program_id(1)
    qi = pl.program_id(2)
    lane = lax.broadcasted_iota(jnp.int32, (1, LANES), 1)
    lo_half = lane < HEAD_DIM
    heads = [(pr, e) for pr in range(pps) for e in range(2)]

    def head_scalar(cc, pr, e):
        lane_c = lax.broadcasted_iota(jnp.int32, cc.shape, 1)
        head = 2 * (hp * pps + pr) + e
        return jnp.sum(jnp.where(lane_c == head, cc, 0.0), axis=1, keepdims=True)

    @pl.when(qi == 0)
    def _():
        cc = ccol_ref[0]
        for pr, e in heads:
            kf = k_ref[0, :, pr * LANES:(pr + 1) * LANES].astype(F32)
            a0 = (1 - e) * HEAD_DIM
            hi, mid, lo = _split3_bf16(-head_scalar(cc, pr, e))
            aug = jnp.where((lane >= a0) & (lane < a0 + 3), 1.0, kf)
            aug = jnp.where(lane == a0 + 3, hi, aug)
            aug = jnp.where(lane == a0 + 4, mid, aug)
            aug = jnp.where(lane == a0 + 5, lo, aug)
            kaug_ref[2 * pr + e] = aug.astype(BF16)

    ct_all = ccol_ref[0, pl.ds(pl.multiple_of(qi * tq, tq), tq), :]
    q_heads = []
    for pr, e in heads:
        qf = q_ref[0, :, pr * LANES:(pr + 1) * LANES].astype(F32)
        a0 = (1 - e) * HEAD_DIM
        hi, mid, lo = _split3_bf16(head_scalar(ct_all, pr, e))
        own = lo_half if e == 0 else jnp.logical_not(lo_half)
        qa = jnp.where(own, qf, 0.0)
        qa = jnp.where(lane == a0, hi, qa)
        qa = jnp.where(lane == a0 + 1, mid, qa)
        qa = jnp.where(lane == a0 + 2, lo, qa)
        qa = jnp.where((lane >= a0 + 3) & (lane < a0 + 6), 1.0, qa)
        q_heads.append(qa.astype(BF16))
    rr = lax.broadcasted_iota(jnp.int32, (tq, tq), 0)
    cc_i = lax.broadcasted_iota(jnp.int32, (tq, tq), 1)
    causal = rr >= cc_i
    nchunk = tq // LANES

    def chunks(a):
        return [a[:, i * LANES:(i + 1) * LANES] for i in range(nchunk)]

    def attend(n):
        outs = []
        for idx, (pr, e) in enumerate(heads):
            s_blocks = []
            for j in range(n + 1):
                kj = kaug_ref[idx, j * tq:(j + 1) * tq, :]
                s = lax.dot_general(q_heads[idx], kj, (((1,), (1,)), ((), ())),
                                    preferred_element_type=F32)
                if j == n:
                    s = jnp.where(causal, s, NEG)
                s_blocks.append(s)
            m_part = functools.reduce(jnp.maximum, [ch for s in s_blocks for ch in chunks(s)])
            m = jnp.max(m_part, axis=1, keepdims=True)
            l_part = jnp.zeros((tq, LANES), F32)
            o = jnp.zeros((tq, LANES), F32)
            for j in range(n + 1):
                p = jnp.exp2(s_blocks[j] - m)
                l_part = l_part + functools.reduce(lambda a, b: a + b, chunks(p))
                vj = v_ref[0, j * tq:(j + 1) * tq, pr * LANES:(pr + 1) * LANES]
                o = o + jnp.dot(p.astype(BF16), vj, preferred_element_type=F32)
            outs.append(o / jnp.sum(l_part, axis=1, keepdims=True))
        for pr in range(pps):
            o_ref[0, :, pr * LANES:(pr + 1) * LANES] = jnp.where(
                lo_half, outs[2 * pr], outs[2 * pr + 1]).astype(o_ref.dtype)

    lax.switch(qi, [functools.partial(attend, n) for n in range(nq)])


FOX_PAIRS_PER_STEP = 4


def _fox_attention(proj, ccol, batch, seq):
    tq = min(256, seq)
    nq = seq // tq
    hd_all = FOX_HEADS * HEAD_DIM
    pps = FOX_PAIRS_PER_STEP
    width = pps * LANES
    groups = hd_all // width
    kern = functools.partial(_fox_kernel, tq=tq, nq=nq, pps=pps)
    return pl.pallas_call(
        kern,
        out_shape=jax.ShapeDtypeStruct((batch, seq, hd_all), BF16),
        grid=(batch, groups, nq),
        in_specs=[pl.BlockSpec((1, tq, width), lambda b, h, i: (b, i, h)),
                  pl.BlockSpec((1, seq, width), lambda b, h, i: (b, 0, groups + h)),
                  pl.BlockSpec((1, seq, width), lambda b, h, i: (b, 0, 2 * groups + h)),
                  pl.BlockSpec((1, seq, LANES), lambda b, h, i: (b, 0, 0))],
        out_specs=pl.BlockSpec((1, tq, width), lambda b, h, i: (b, i, h)),
        scratch_shapes=[pltpu.VMEM((2 * pps, seq, LANES), BF16)],
        compiler_params=_cparams("parallel", "parallel", "arbitrary"),
        name="fox_attention",
    )(proj, proj, proj, ccol)


def _layer_norm(y, gain, bias):
    mu = jnp.mean(y, axis=-1, keepdims=True)
    d = y - mu
    var = jnp.mean(d * d, axis=-1, keepdims=True)
    return d * lax.rsqrt(var + LN_EPS) * gain + bias


def _split_bf16(v):
    hi = v.astype(BF16)
    lo = (v - hi.astype(F32)).astype(BF16)
    return hi, lo


def _route(logits_t):
    rows = [logits_t[e:e + 1, :] for e in range(N_EXPERTS)]
    mx = functools.reduce(jnp.maximum, rows)
    ex = [jnp.exp(v - mx) for v in rows]
    den = functools.reduce(lambda a, b: a + b, ex)
    sc = [v / den for v in ex]
    gscore = []
    for g in range(N_GROUPS):
        mem = sc[g * EXPERTS_PER_GROUP:(g + 1) * EXPERTS_PER_GROUP]
        pairs = [mem[i] + mem[j] for i in range(4) for j in range(i + 1, 4)]
        gscore.append(functools.reduce(jnp.maximum, pairs))
    gmax = functools.reduce(jnp.maximum, gscore)
    taken = None
    gates = []
    for g in range(N_GROUPS):
        eq = gscore[g] == gmax
        best = eq if taken is None else eq & jnp.logical_not(taken)
        taken = eq if taken is None else taken | eq
        mem = sc[g * EXPERTS_PER_GROUP:(g + 1) * EXPERTS_PER_GROUP]
        sel = []
        for i in range(4):
            rank = jnp.zeros_like(mem[i])
            for j in range(4):
                if j == i:
                    continue
                ahead = (mem[j] > mem[i]) | ((mem[j] == mem[i]) & (j < i))
                rank = rank + jnp.where(ahead, 1.0, 0.0)
            sel.append(best & (rank < 2.0))
        tot = functools.reduce(lambda a, b: a + b,
                               [jnp.where(sel[i], mem[i], 0.0) for i in range(4)])
        for i in range(4):
            gates.append(jnp.where(sel[i], mem[i] / tot, 0.0))
    return jnp.concatenate(gates, axis=0)


def _outproj_kernel(a_ref, wo_ref, x_ref, g_ref, b_ref, wrh_ref, wrl_ref, br_ref,
                    x1_ref, gate_ref):
    mix = jnp.dot(a_ref[...], wo_ref[...], preferred_element_type=F32)
    x1 = _layer_norm(ALPHA * x_ref[...] + mix, g_ref[...], b_ref[...])
    x1_ref[...] = x1
    hi, lo = _split_bf16(x1)
    logits = (jnp.dot(hi, wrh_ref[...], preferred_element_type=F32)
              + (jnp.dot(lo, wrh_ref[...], preferred_element_type=F32)
                 + jnp.dot(hi, wrl_ref[...], preferred_element_type=F32))) + br_ref[...]
    gates_t = _route(logits.T[:N_EXPERTS, :])
    pad = jnp.zeros((LANES - N_EXPERTS, gates_t.shape[1]), F32)
    gate_ref[...] = jnp.concatenate([gates_t, pad], axis=0).T


def _outproj_ln_router(a2, wo, x2, gain, bias, wr_hi, wr_lo, br):
    t_tokens, d = x2.shape
    tm = min(512, t_tokens)
    row = lambda i: (i, 0)
    fixed = lambda i: (0, 0)
    return pl.pallas_call(
        _outproj_kernel,
        out_shape=(jax.ShapeDtypeStruct((t_tokens, d), F32),
                   jax.ShapeDtypeStruct((t_tokens, LANES), F32)),
        grid=(t_tokens // tm,),
        in_specs=[pl.BlockSpec((tm, d), row),
                  pl.BlockSpec((d, d), fixed),
                  pl.BlockSpec((tm, d), row),
                  pl.BlockSpec((1, d), fixed),
                  pl.BlockSpec((1, d), fixed),
                  pl.BlockSpec((d, LANES), fixed),
                  pl.BlockSpec((d, LANES), fixed),
                  pl.BlockSpec((1, LANES), fixed)],
        out_specs=(pl.BlockSpec((tm, d), row), pl.BlockSpec((tm, LANES), row)),
        compiler_params=_cparams("parallel"),
        name="outproj_ln_router",
    )(a2, wo, x2, gain, bias, wr_hi, wr_lo, br)


def _moe_kernel(x_ref, gate_ref, wgu_ref, wd_ref, g_ref, b_ref, o_ref, xb_ref, acc_ref):
    grp = pl.program_id(1)

    @pl.when(grp == 0)
    def _():
        xb_ref[...] = x_ref[...].astype(BF16)
        acc_ref[...] = jnp.zeros_like(acc_ref)

    gates = gate_ref[...]
    lane = lax.broadcasted_iota(jnp.int32, gates.shape, 1)
    for k in range(EXPERTS_PER_GROUP):
        h = jnp.dot(xb_ref[...], wgu_ref[k], preferred_element_type=F32)
        hg, hu = h[:, :D_EXPERT], h[:, D_EXPERT:]
        ge = jnp.sum(jnp.where(lane == grp * EXPERTS_PER_GROUP + k, gates, 0.0),
                     axis=1, keepdims=True)
        act = hg * jax.nn.sigmoid(hg) * hu * ge
        acc_ref[...] += jnp.dot(act.astype(BF16), wd_ref[k], preferred_element_type=F32)

    @pl.when(grp == N_GROUPS - 1)
    def _():
        o_ref[...] = _layer_norm(ALPHA * x_ref[...] + acc_ref[...], g_ref[...], b_ref[...])


def _moe_ln(x2, gates, wgu, wd, gain, bias):
    t_tokens, d = x2.shape
    tm = min(1024, t_tokens)
    row = lambda i, e: (i, 0)
    fixed = lambda i, e: (0, 0)
    return pl.pallas_call(
        _moe_kernel,
        out_shape=jax.ShapeDtypeStruct((t_tokens, d), F32),
        grid=(t_tokens // tm, N_GROUPS),
        in_specs=[pl.BlockSpec((tm, d), row),
                  pl.BlockSpec((tm, LANES), row),
                  pl.BlockSpec((EXPERTS_PER_GROUP, d, 2 * D_EXPERT), lambda i, e: (e, 0, 0)),
                  pl.BlockSpec((EXPERTS_PER_GROUP, D_EXPERT, d), lambda i, e: (e, 0, 0)),
                  pl.BlockSpec((1, d), fixed),
                  pl.BlockSpec((1, d), fixed)],
        out_specs=pl.BlockSpec((tm, d), row),
        scratch_shapes=[pltpu.VMEM((tm, d), BF16), pltpu.VMEM((tm, d), F32)],
        compiler_params=_cparams("parallel", "arbitrary"),
        name="moe_ln",
    )(x2, gates, wgu, wd, gain, bias)


def _pad_lanes(w):
    return jnp.pad(w, ((0, 0), (0, LANES - w.shape[1])))


def kernel(x, ln_gain, ln_bias, swa_w_in, swa_sinks, swa_w_o, fox_w_in, fox_b_f, fox_w_o,
           w_router, b_router, w_gate, w_up, w_down):
    batch, seq, d = x.shape
    t_tokens = batch * seq
    tables = _rope_tables(seq)
    wr_hi, wr_lo = _split_bf16(_pad_lanes(w_router))
    br = _pad_lanes(b_router[None, :])
    fox_qkv = 3 * FOX_HEADS * HEAD_DIM
    x2 = x.reshape(t_tokens, d)
    for i in range(DEPTH):
        j = i // 2
        if i % 2 == 0:
            proj = _in_projection(x2, swa_w_in[j].astype(BF16), seq,
                                  q_cols=SWA_Q_HEADS * HEAD_DIM, q_scale=QK_SCALE,
                                  rope_cols=(SWA_Q_HEADS + SWA_KV_HEADS) * HEAD_DIM,
                                  tables=tables)
            attn = _swa_attention(proj.reshape(batch, seq, -1), swa_sinks[j], batch, seq)
            w_o = swa_w_o[j]
        else:
            w_in = fox_w_in[j]
            proj = _in_projection(x2, w_in[:, :fox_qkv].astype(BF16), seq,
                                  q_cols=FOX_HEADS * HEAD_DIM, q_scale=QK_SCALE * LOG2E,
                                  rope_cols=0, tables=tables)
            ccol = _decay_cumsum(x2.reshape(batch, seq, d),
                                 _pad_lanes(w_in[:, fox_qkv:]).astype(BF16),
                                 _pad_lanes(fox_b_f[j][None, :]))
            attn = _fox_attention(proj.reshape(batch, seq, -1), ccol, batch, seq)
            w_o = fox_w_o[j]
        x1, gates = _outproj_ln_router(attn.reshape(t_tokens, d), w_o.astype(BF16), x2,
                                       ln_gain[i, 0][None, :], ln_bias[i, 0][None, :],
                                       wr_hi, wr_lo, br)
        wgu = jnp.concatenate([w_gate[i], w_up[i]], axis=-1).astype(BF16)
        x2 = _moe_ln(x1, gates, wgu, w_down[i].astype(BF16),
                     ln_gain[i, 1][None, :], ln_bias[i, 1][None, :])
    return x2.reshape(batch, seq, d)
```

```python
import functools
import math

import jax
import jax.numpy as jnp
from jax import lax
from jax.experimental import pallas as pl
from jax.experimental.pallas import tpu as pltpu

F32 = jnp.float32
BF16 = jnp.bfloat16

HEAD_DIM = 64
ROT_DIM = 16
ROT_HALF = ROT_DIM // 2
ROPE_THETA = 500000.0
SWA_Q_HEADS = 16
SWA_KV_HEADS = 4
SWA_BLOCK = 128
FOX_HEADS = 16
N_EXPERTS = 16
N_GROUPS = 4
EXPERTS_PER_GROUP = 4
D_EXPERT = 256
DEPTH = 4
ALPHA = (2.0 * DEPTH) ** 0.25
LN_EPS = 1e-5
NEG = -1e30
QK_SCALE = HEAD_DIM ** -0.5
LOG2E = math.log2(math.e)

LANES = 128
VMEM_LIMIT = 56 * 1024 * 1024


def _cparams(*sem):
    return pltpu.CompilerParams(dimension_semantics=sem, vmem_limit_bytes=VMEM_LIMIT)


def _inproj_kernel(x_ref, w_ref, cos_ref, sa_ref, sb_ref, o_ref, *, tn, q_cols, q_scale,
                   rope_cols):
    xb = x_ref[...].astype(BF16)
    n = w_ref.shape[1]
    for c0 in range(0, n, tn):
        acc = jnp.dot(xb, w_ref[:, c0:c0 + tn], preferred_element_type=F32)
        n_rope = min(max(rope_cols - c0, 0), tn)
        if n_rope:
            reps = n_rope // LANES
            head = acc[:, :n_rope]
            head = (head * jnp.tile(cos_ref[...], (1, reps))
                    + pltpu.roll(head, ROT_HALF, 1) * jnp.tile(sa_ref[...], (1, reps))
                    + pltpu.roll(head, n_rope - ROT_HALF, 1) * jnp.tile(sb_ref[...], (1, reps)))
            acc = head if n_rope == tn else jnp.concatenate([head, acc[:, n_rope:]], axis=1)
        if c0 + tn <= q_cols:
            acc = acc * q_scale
        else:
            assert c0 >= q_cols, "q columns must end on a chunk boundary"
        o_ref[:, c0:c0 + tn] = acc.astype(o_ref.dtype)


def _rope_tables(seq):
    inv_freq = jnp.power(ROPE_THETA, -jnp.arange(ROT_HALF, dtype=F32) * (2.0 / ROT_DIM))
    ang = jnp.arange(seq, dtype=F32)[:, None] * inv_freq[None, :]
    cos, sin = jnp.cos(ang), jnp.sin(ang)
    ones = jnp.ones((seq, HEAD_DIM - ROT_DIM), F32)
    zeros = jnp.zeros((seq, HEAD_DIM - ROT_DIM), F32)
    z8 = jnp.zeros((seq, ROT_HALF), F32)
    cos_h = jnp.concatenate([cos, cos, ones], axis=1)
    sa_h = jnp.concatenate([z8, sin, zeros], axis=1)
    sb_h = jnp.concatenate([-sin, z8, zeros], axis=1)
    rep = LANES // HEAD_DIM
    return tuple(jnp.tile(t, (1, rep)) for t in (cos_h, sa_h, sb_h))


def _in_projection(x2, w, seq, *, q_cols, q_scale, rope_cols, tables):
    t_tokens, d = x2.shape
    n = w.shape[1]
    tm = min(1024, seq)
    tn = 512
    assert t_tokens % tm == 0 and seq % tm == 0 and n % tn == 0
    pos_blocks = seq // tm
    kern = functools.partial(_inproj_kernel, tn=tn, q_cols=q_cols, q_scale=q_scale,
                             rope_cols=rope_cols)
    tab_spec = pl.BlockSpec((tm, LANES), lambda i: (i % pos_blocks, 0))
    return pl.pallas_call(
        kern,
        out_shape=jax.ShapeDtypeStruct((t_tokens, n), BF16),
        grid=(t_tokens // tm,),
        in_specs=[pl.BlockSpec((tm, d), lambda i: (i, 0)),
                  pl.BlockSpec((d, n), lambda i: (0, 0)),
                  tab_spec, tab_spec, tab_spec],
        out_specs=pl.BlockSpec((tm, n), lambda i: (i, 0)),
        compiler_params=_cparams("parallel"),
        name="in_projection",
    )(x2, w, *tables)


def _swa_kernel(sink_ref, q_ref, kp_ref, kc_ref, vp_ref, vc_ref, o_ref):
    n = pl.program_id(1)
    blk = SWA_BLOCK
    lane = lax.broadcasted_iota(jnp.int32, (1, LANES), 1)
    lo = lane < HEAD_DIM
    r = lax.broadcasted_iota(jnp.int32, (blk, 2 * blk), 0)
    c = lax.broadcasted_iota(jnp.int32, (blk, 2 * blk), 1)
    allowed = (c > r) & (c <= r + blk) & ((c >= blk) | (n > 0))
    k_all = jnp.concatenate([kp_ref[0], kc_ref[0]], axis=0).astype(F32)
    v_all = jnp.concatenate([vp_ref[0], vc_ref[0]], axis=0).astype(F32)
    for g in range(SWA_KV_HEADS):
        grp = g // 2
        kf = k_all[:, grp * LANES:(grp + 1) * LANES]
        vf = v_all[:, grp * LANES:(grp + 1) * LANES]
        kr = pltpu.roll(kf, HEAD_DIM, 1)
        vr = pltpu.roll(vf, HEAD_DIM, 1)
        own_lo = (g % 2 == 0)
        k_src_lo, k_src_hi = (kf, kr) if own_lo else (kr, kf)
        k_lo = jnp.where(lo, k_src_lo, 0.0).astype(BF16)
        k_hi = jnp.where(lo, 0.0, k_src_hi).astype(BF16)
        v_dup = jnp.where(lo, vf if own_lo else vr, vr if own_lo else vf).astype(BF16)
        for pp in range(2):
            p = 2 * g + pp
            qp = q_ref[0, :, p * LANES:(p + 1) * LANES]
            outs = []
            for e, kk in enumerate((k_lo, k_hi)):
                s = lax.dot_general(qp, kk, (((1,), (1,)), ((), ())),
                                    preferred_element_type=F32)
                s = jnp.where(allowed, s, NEG)
                sink = sink_ref[2 * p + e]
                m = jnp.maximum(jnp.max(s, axis=1, keepdims=True), sink)
                pr = jnp.exp(s - m)
                den = jnp.sum(pr, axis=1, keepdims=True) + jnp.exp(sink - m)
                o = jnp.dot(pr.astype(BF16), v_dup, preferred_element_type=F32)
                outs.append(o / den)
            o_ref[0, :, p * LANES:(p + 1) * LANES] = jnp.where(lo, outs[0], outs[1]).astype(o_ref.dtype)


def _swa_attention(proj, sinks, batch, seq):
    blk = SWA_BLOCK
    nb = seq // blk
    qd = SWA_Q_HEADS * HEAD_DIM
    kd = SWA_KV_HEADS * HEAD_DIM
    kcol, vcol = qd // kd, qd // kd + 1
    prev = lambda b, n: jnp.maximum(n - 1, 0)
    return pl.pallas_call(
        _swa_kernel,
        out_shape=jax.ShapeDtypeStruct((batch, seq, qd), BF16),
        grid=(batch, nb),
        in_specs=[pl.BlockSpec(memory_space=pltpu.SMEM),
                  pl.BlockSpec((1, blk, qd), lambda b, n: (b, n, 0)),
                  pl.BlockSpec((1, blk, kd), lambda b, n: (b, prev(b, n), kcol)),
                  pl.BlockSpec((1, blk, kd), lambda b, n: (b, n, kcol)),
                  pl.BlockSpec((1, blk, kd), lambda b, n: (b, prev(b, n), vcol)),
                  pl.BlockSpec((1, blk, kd), lambda b, n: (b, n, vcol))],
        out_specs=pl.BlockSpec((1, blk, qd), lambda b, n: (b, n, 0)),
        compiler_params=_cparams("parallel", "arbitrary"),
        name="swa_attention",
    )(sinks, proj, proj, proj, proj, proj)


def _decay_kernel(x_ref, wf_ref, bf_ref, ccol_ref):
    seq = x_ref.shape[1]
    f = jnp.dot(x_ref[0].astype(BF16), wf_ref[...], preferred_element_type=F32) + bf_ref[...]
    ls = jnp.minimum(f, 0.0) - jnp.log1p(jnp.exp(-jnp.abs(f)))
    row = lax.broadcasted_iota(jnp.int32, ls.shape, 0)
    c = ls
    k = 1
    while k < seq:
        c = c + jnp.where(row >= k, pltpu.roll(c, k, 0), 0.0)
        k *= 2
    ccol_ref[0] = c * LOG2E


def _decay_cumsum(x3, wf, bfv):
    batch, seq, d = x3.shape
    return pl.pallas_call(
        _decay_kernel,
        out_shape=jax.ShapeDtypeStruct((batch, seq, LANES), F32),
        grid=(batch,),
        in_specs=[pl.BlockSpec((1, seq, d), lambda b: (b, 0, 0)),
                  pl.BlockSpec((d, LANES), lambda b: (0, 0)),
                  pl.BlockSpec((1, LANES), lambda b: (0, 0))],
        out_specs=pl.BlockSpec((1, seq, LANES), lambda b: (b, 0, 0)),
        compiler_params=_cparams("parallel"),
        name="fox_decay_cumsum",
    )(x3, wf, bfv)


def _split3_bf16(v):
    hi = v.astype(BF16).astype(F32)
    r = v - hi
    mid = r.astype(BF16).astype(F32)
    lo = (r - mid).astype(BF16).astype(F32)
    return hi, mid, lo


def _fox_kernel(q_ref, k_ref, v_ref, ccol_ref, o_ref, kaug_ref, *, tq, nq, pps):
    hp = pl.program_id(1)
    qi = pl.program_id(2)
    lane = lax.broadcasted_iota(jnp.int32, (1, LANES), 1)
    lo_half = lane < HEAD_DIM
    heads = [(pr, e) for pr in range(pps) for e in range(2)]

    def head_scalar(cc, pr, e):
        lane_c = lax.broadcasted_iota(jnp.int32, cc.shape, 1)
        head = 2 * (hp * pps + pr) + e
        return jnp.sum(jnp.where(lane_c == head, cc, 0.0), axis=1, keepdims=True)

    @pl.when(qi == 0)
    def _():
        cc = ccol_ref[0]
        for pr, e in heads:
            kf = k_ref[0, :, pr * LANES:(pr + 1) * LANES].astype(F32)
            a0 = (1 - e) * HEAD_DIM
            hi, mid, lo = _split3_bf16(-head_scalar(cc, pr, e))
            aug = jnp.where((lane >= a0) & (lane < a0 + 3), 1.0, kf)
            aug = jnp.where(lane == a0 + 3, hi, aug)
            aug = jnp.where(lane == a0 + 4, mid, aug)
            aug = jnp.where(lane == a0 + 5, lo, aug)
            kaug_ref[2 * pr + e] = aug.astype(BF16)

    ct_all = ccol_ref[0, pl.ds(pl.multiple_of(qi * tq, tq), tq), :]
    q_heads = []
    for pr, e in heads:
        qf = q_ref[0, :, pr * LANES:(pr + 1) * LANES].astype(F32)
        a0 = (1 - e) * HEAD_DIM
        hi, mid, lo = _split3_bf16(head_scalar(ct_all, pr, e))
        own = lo_half if e == 0 else jnp.logical_not(lo_half)
        qa = jnp.where(own, qf, 0.0)
        qa = jnp.where(lane == a0, hi, qa)
        qa = jnp.where(lane == a0 + 1, mid, qa)
        qa = jnp.where(lane == a0 + 2, lo, qa)
        qa = jnp.where((lane >= a0 + 3) & (lane < a0 + 6), 1.0, qa)
        q_heads.append(qa.astype(BF16))
    rr = lax.broadcasted_iota(jnp.int32, (tq, tq), 0)
    cc_i = lax.broadcasted_iota(jnp.int32, (tq, tq), 1)
    causal = rr >= cc_i
    nchunk = tq // LANES

    def chunks(a):
        return [a[:, i * LANES:(i + 1) * LANES] for i in range(nchunk)]

    def attend(n):
        outs = []
        for idx, (pr, e) in enumerate(heads):
            s_blocks = []
            for j in range(n + 1):
                kj = kaug_ref[idx, j * tq:(j + 1) * tq, :]
                s = lax.dot_general(q_heads[idx], kj, (((1,), (1,)), ((), ())),
                                    preferred_element_type=F32)
                if j == n:
                    s = jnp.where(causal, s, NEG)
                s_blocks.append(s)
            m_part = functools.reduce(jnp.maximum, [ch for s in s_blocks for ch in chunks(s)])
            m = jnp.max(m_part, axis=1, keepdims=True)
            l_part = jnp.zeros((tq, LANES), F32)
            o = jnp.zeros((tq, LANES), F32)
            for j in range(n + 1):
                p = jnp.exp2(s_blocks[j] - m)
                l_part = l_part + functools.reduce(lambda a, b: a + b, chunks(p))
                vj = v_ref[0, j * tq:(j + 1) * tq, pr * LANES:(pr + 1) * LANES]
                o = o + jnp.dot(p.astype(BF16), vj, preferred_element_type=F32)
            outs.append(o / jnp.sum(l_part, axis=1, keepdims=True))
        for pr in range(pps):
            o_ref[0, :, pr * LANES:(pr + 1) * LANES] = jnp.where(
                lo_half, outs[2 * pr], outs[2 * pr + 1]).astype(o_ref.dtype)

    lax.switch(qi, [functools.partial(attend, n) for n in range(nq)])


FOX_PAIRS_PER_STEP = 4


def _fox_attention(proj, ccol, batch, seq):
    tq = min(256, seq)
    nq = seq // tq
    hd_all = FOX_HEADS * HEAD_DIM
    pps = FOX_PAIRS_PER_STEP
    width = pps * LANES
    groups = hd_all // width
    kern = functools.partial(_fox_kernel, tq=tq, nq=nq, pps=pps)
    return pl.pallas_call(
        kern,
        out_shape=jax.ShapeDtypeStruct((batch, seq, hd_all), BF16),
        grid=(batch, groups, nq),
        in_specs=[pl.BlockSpec((1, tq, width), lambda b, h, i: (b, i, h)),
                  pl.BlockSpec((1, seq, width), lambda b, h, i: (b, 0, groups + h)),
                  pl.BlockSpec((1, seq, width), lambda b, h, i: (b, 0, 2 * groups + h)),
                  pl.BlockSpec((1, seq, LANES), lambda b, h, i: (b, 0, 0))],
        out_specs=pl.BlockSpec((1, tq, width), lambda b, h, i: (b, i, h)),
        scratch_shapes=[pltpu.VMEM((2 * pps, seq, LANES), BF16)],
        compiler_params=_cparams("parallel", "parallel", "arbitrary"),
        name="fox_attention",
    )(proj, proj, proj, ccol)


def _layer_norm(y, gain, bias):
    mu = jnp.mean(y, axis=-1, keepdims=True)
    d = y - mu
    var = jnp.mean(d * d, axis=-1, keepdims=True)
    return d * lax.rsqrt(var + LN_EPS) * gain + bias


def _split_bf16(v):
    hi = v.astype(BF16)
    lo = (v - hi.astype(F32)).astype(BF16)
    return hi, lo


EXPERT_PAIRS = [(i, j) for i in range(EXPERTS_PER_GROUP) for j in range(i + 1, EXPERTS_PER_GROUP)]
N_BUCKETS = N_GROUPS * len(EXPERT_PAIRS)


def _route(logits_t):
    rows = [logits_t[e:e + 1, :] for e in range(N_EXPERTS)]
    mx = functools.reduce(jnp.maximum, rows)
    ex = [jnp.exp(v - mx) for v in rows]
    den = functools.reduce(lambda a, b: a + b, ex)
    sc = [v / den for v in ex]
    gscore = []
    for g in range(N_GROUPS):
        mem = sc[g * EXPERTS_PER_GROUP:(g + 1) * EXPERTS_PER_GROUP]
        pairs = [mem[i] + mem[j] for i in range(4) for j in range(i + 1, 4)]
        gscore.append(functools.reduce(jnp.maximum, pairs))
    gmax = functools.reduce(jnp.maximum, gscore)
    taken = None
    bucket = jnp.zeros_like(gmax)
    gate_a = jnp.zeros_like(gmax)
    gate_b = jnp.zeros_like(gmax)
    for g in range(N_GROUPS):
        eq = gscore[g] == gmax
        best = eq if taken is None else eq & jnp.logical_not(taken)
        taken = eq if taken is None else taken | eq
        mem = sc[g * EXPERTS_PER_GROUP:(g + 1) * EXPERTS_PER_GROUP]
        sel = []
        for i in range(4):
            rank = jnp.zeros_like(mem[i])
            for j in range(4):
                if j == i:
                    continue
                ahead = (mem[j] > mem[i]) | ((mem[j] == mem[i]) & (j < i))
                rank = rank + jnp.where(ahead, 1.0, 0.0)
            sel.append(best & (rank < 2.0))
        tot = functools.reduce(lambda a, b: a + b,
                               [jnp.where(sel[i], mem[i], 0.0) for i in range(4)])
        seen = jnp.zeros_like(gmax)
        for i in range(4):
            gate = mem[i] / tot
            gate_a = gate_a + jnp.where(sel[i] & (seen == 0.0), gate, 0.0)
            gate_b = gate_b + jnp.where(sel[i] & (seen == 1.0), gate, 0.0)
            seen = seen + jnp.where(sel[i], 1.0, 0.0)
        for p, (i, j) in enumerate(EXPERT_PAIRS):
            bucket = bucket + jnp.where(sel[i] & sel[j], float(g * len(EXPERT_PAIRS) + p), 0.0)
    return bucket, gate_a, gate_b


OUTPROJ_SUBTILES = 2


def _outproj_kernel(a_ref, wo_ref, x_ref, g_ref, b_ref, wr_ref, br_ref, x1_ref, gate_ref):
    sub = x_ref.shape[0] // OUTPROJ_SUBTILES
    for h in range(OUTPROJ_SUBTILES):
        rows = slice(h * sub, (h + 1) * sub)
        mix = jnp.dot(a_ref[rows, :], wo_ref[...], preferred_element_type=F32)
        x1 = _layer_norm(ALPHA * x_ref[rows, :] + mix, g_ref[...], b_ref[...])
        x1_ref[rows, :] = x1
        hi, lo = _split_bf16(x1)
        from_hi = jnp.dot(hi, wr_ref[...], preferred_element_type=F32)
        from_lo = jnp.dot(lo, wr_ref[...], preferred_element_type=F32)
        logits = (from_hi[:, :LANES] + (from_lo[:, :LANES] + from_hi[:, LANES:])) + br_ref[...]
        routed = _route(logits.T[:N_EXPERTS, :])
        pad = jnp.zeros((LANES - len(routed), sub), F32)
        gate_ref[rows, :] = jnp.concatenate(list(routed) + [pad], axis=0).T


def _outproj_ln_router(a2, wo, x2, gain, bias, wr, br):
    t_tokens, d = x2.shape
    tm = min(512, t_tokens)
    row = lambda i: (i, 0)
    fixed = lambda i: (0, 0)
    return pl.pallas_call(
        _outproj_kernel,
        out_shape=(jax.ShapeDtypeStruct((t_tokens, d), F32),
                   jax.ShapeDtypeStruct((t_tokens, LANES), F32)),
        grid=(t_tokens // tm,),
        in_specs=[pl.BlockSpec((tm, d), row),
                  pl.BlockSpec((d, d), fixed),
                  pl.BlockSpec((tm, d), row),
                  pl.BlockSpec((1, d), fixed),
                  pl.BlockSpec((1, d), fixed),
                  pl.BlockSpec((d, 2 * LANES), fixed),
                  pl.BlockSpec((1, LANES), fixed)],
        out_specs=(pl.BlockSpec((tm, d), row), pl.BlockSpec((tm, LANES), row)),
        compiler_params=_cparams("parallel"),
        name="outproj_ln_router",
    )(a2, wo, x2, gain, bias, wr, br)


MOE_TILE = 256
ROW_W = 1024 + LANES


def _rank_kernel(route_ref, rank_ref, cnt_ref, carry_ref):
    i = pl.program_id(0)

    @pl.when(i == 0)
    def _():
        carry_ref[...] = jnp.zeros_like(carry_ref)

    route = route_ref[...]
    tile = route.shape[0]
    lane = lax.broadcasted_iota(jnp.int32, route.shape, 1).astype(F32)
    onehot = lane == route[:, 0:1]
    oh = jnp.where(onehot, 1.0, 0.0)
    rr = lax.broadcasted_iota(jnp.int32, (tile, tile), 0)
    cc = lax.broadcasted_iota(jnp.int32, (tile, tile), 1)
    earlier = jnp.where(rr > cc, 1.0, 0.0).astype(BF16)
    before = jnp.dot(earlier, oh.astype(BF16), preferred_element_type=F32) + carry_ref[...]
    rank = jnp.sum(jnp.where(onehot, before, 0.0), axis=1, keepdims=True)
    rank_ref[...] = jnp.broadcast_to(rank, route.shape)
    carry_ref[...] += jnp.sum(oh, axis=0, keepdims=True)
    cnt_ref[...] = carry_ref[...]


def _bucket_ranks(route):
    t_tokens = route.shape[0]
    tile = min(1024, t_tokens)
    return pl.pallas_call(
        _rank_kernel,
        out_shape=(jax.ShapeDtypeStruct((t_tokens, LANES), F32),
                   jax.ShapeDtypeStruct((1, LANES), F32)),
        grid=(t_tokens // tile,),
        in_specs=[pl.BlockSpec((tile, LANES), lambda i: (i, 0))],
        out_specs=(pl.BlockSpec((tile, LANES), lambda i: (i, 0)),
                   pl.BlockSpec((1, LANES), lambda i: (0, 0))),
        scratch_shapes=[pltpu.VMEM((1, LANES), F32)],
        compiler_params=_cparams("arbitrary"),
        name="moe_bucket_ranks",
    )(route)


def _sort_plan(route, rank, cnt, n_tiles):
    counts = cnt[0, :N_BUCKETS].astype(jnp.int32)
    padded = (counts + MOE_TILE - 1) // MOE_TILE * MOE_TILE
    ends = jnp.cumsum(padded)
    starts = ends - padded
    bucket = route[:, 0].astype(jnp.int32)
    pos = starts[bucket] + rank[:, 0].astype(jnp.int32)
    n_used = ends[-1] // MOE_TILE
    tile_start = jnp.arange(n_tiles, dtype=jnp.int32) * MOE_TILE
    tile_bucket = jnp.minimum(jnp.sum(ends[None, :] <= tile_start[:, None], axis=1), N_BUCKETS - 1)
    pair_a = jnp.array([p[0] for p in EXPERT_PAIRS], jnp.int32)
    pair_b = jnp.array([p[1] for p in EXPERT_PAIRS], jnp.int32)
    grp = tile_bucket // len(EXPERT_PAIRS)
    pair = tile_bucket % len(EXPERT_PAIRS)
    tile_ea = grp * EXPERTS_PER_GROUP + pair_a[pair]
    tile_eb = grp * EXPERTS_PER_GROUP + pair_b[pair]
    tile_src = jnp.minimum(jnp.arange(n_tiles, dtype=jnp.int32), n_used - 1)
    return pos, ends, padded, tile_ea, tile_eb, tile_src, n_used.reshape(1)


def _scatter_kernel(pos_ref, ends_ref, padded_ref, x_ref, route_ref, xs_ref, stage_ref, zero_ref,
                    sem, zsem):
    i = pl.program_id(0)
    tm = x_ref.shape[0]

    @pl.when(i == 0)
    def _():
        zero_ref[...] = jnp.zeros_like(zero_ref)
        n_slots = xs_ref.shape[0]
        tail_start = n_slots - N_BUCKETS * MOE_TILE

        def fill(start):
            return pltpu.make_async_copy(zero_ref, xs_ref.at[pl.ds(start, MOE_TILE), :], zsem)

        def bucket_tail(b):
            return pl.multiple_of(ends_ref[b] - MOE_TILE, MOE_TILE)

        def bucket_needs_fill(b):
            return (padded_ref[b] > 0) & (ends_ref[b] <= tail_start)

        for start in range(tail_start, n_slots, MOE_TILE):
            fill(start).start()
        for b in range(N_BUCKETS):
            @pl.when(bucket_needs_fill(b))
            def _():
                fill(bucket_tail(b)).start()
        for start in range(tail_start, n_slots, MOE_TILE):
            fill(start).wait()
        for b in range(N_BUCKETS):
            @pl.when(bucket_needs_fill(b))
            def _():
                fill(bucket_tail(b)).wait()

    stage_ref[:, :x_ref.shape[1]] = x_ref[...]
    stage_ref[:, x_ref.shape[1]:] = route_ref[...]

    def row_copy(r):
        slot = pos_ref[i * tm + r]
        return pltpu.make_async_copy(stage_ref.at[pl.ds(r, 1), :], xs_ref.at[pl.ds(slot, 1), :], sem)

    def start(r, carry):
        row_copy(r).start()
        return carry

    def wait(r, carry):
        row_copy(r).wait()
        return carry

    lax.fori_loop(0, tm, start, 0, unroll=8)
    lax.fori_loop(0, tm, wait, 0, unroll=8)


def _scatter_rows(pos, ends, padded, x1, route, n_slots):
    t_tokens, d = x1.shape
    tm = min(512, t_tokens)
    grid_spec = pltpu.PrefetchScalarGridSpec(
        num_scalar_prefetch=3,
        grid=(t_tokens // tm,),
        in_specs=[pl.BlockSpec((tm, d), lambda i, *_: (i, 0)),
                  pl.BlockSpec((tm, LANES), lambda i, *_: (i, 0))],
        out_specs=pl.BlockSpec(memory_space=pl.ANY),
        scratch_shapes=[pltpu.VMEM((tm, ROW_W), F32), pltpu.VMEM((MOE_TILE, ROW_W), F32),
                        pltpu.SemaphoreType.DMA, pltpu.SemaphoreType.DMA],
    )
    return pl.pallas_call(
        _scatter_kernel,
        out_shape=jax.ShapeDtypeStruct((n_slots, ROW_W), F32),
        grid_spec=grid_spec,
        compiler_params=_cparams("arbitrary"),
        name="moe_scatter_rows",
    )(pos, ends, padded, x1, route)


def _expert_kernel(ea_ref, eb_ref, src_ref, nu_ref, xs_ref, wga_ref, wua_ref, wda_ref,
                   wgb_ref, wub_ref, wdb_ref, ys_ref):
    k = pl.program_id(0)
    d = ys_ref.shape[1]

    @pl.when(k < nu_ref[0])
    def _():
        xb = xs_ref[:, :d].astype(BF16)
        y = jnp.zeros(ys_ref.shape, F32)
        for lane, (wg, wu, wd) in ((1, (wga_ref, wua_ref, wda_ref)), (2, (wgb_ref, wub_ref, wdb_ref))):
            gate = xs_ref[:, d + lane:d + lane + 1]
            hg = jnp.dot(xb, wg[0].astype(BF16), preferred_element_type=F32)
            hu = jnp.dot(xb, wu[0].astype(BF16), preferred_element_type=F32)
            act = hg * jax.nn.sigmoid(hg) * hu * gate
            y = y + jnp.dot(act.astype(BF16), wd[0].astype(BF16), preferred_element_type=F32)
        ys_ref[...] = y

    @pl.when(k >= nu_ref[0])
    def _():
        ys_ref[...] = jnp.zeros_like(ys_ref)


def _run_experts(tile_ea, tile_eb, tile_src, n_used, xs, w_gate, w_up, w_down):
    n_slots = xs.shape[0]
    d = w_gate.shape[1]
    n_tiles = n_slots // MOE_TILE
    up_a = pl.BlockSpec((1, d, D_EXPERT), lambda k, ea, eb, src, nu: (ea[k], 0, 0))
    up_b = pl.BlockSpec((1, d, D_EXPERT), lambda k, ea, eb, src, nu: (eb[k], 0, 0))
    down_a = pl.BlockSpec((1, D_EXPERT, d), lambda k, ea, eb, src, nu: (ea[k], 0, 0))
    down_b = pl.BlockSpec((1, D_EXPERT, d), lambda k, ea, eb, src, nu: (eb[k], 0, 0))
    grid_spec = pltpu.PrefetchScalarGridSpec(
        num_scalar_prefetch=4,
        grid=(n_tiles,),
        in_specs=[pl.BlockSpec((MOE_TILE, ROW_W), lambda k, ea, eb, src, nu: (src[k], 0)),
                  up_a, up_a, down_a, up_b, up_b, down_b],
        out_specs=pl.BlockSpec((MOE_TILE, d), lambda k, ea, eb, src, nu: (k, 0)),
    )
    return pl.pallas_call(
        _expert_kernel,
        out_shape=jax.ShapeDtypeStruct((n_slots, d), F32),
        grid_spec=grid_spec,
        compiler_params=_cparams("arbitrary"),
        name="moe_experts",
    )(tile_ea, tile_eb, tile_src, n_used, xs, w_gate, w_up, w_down, w_gate, w_up, w_down)


def _gather_ln_kernel(pos_ref, x_ref, ys_ref, g_ref, b_ref, o_ref, ybuf_ref, sem):
    i = pl.program_id(0)
    tm = x_ref.shape[0]

    def row_copy(r):
        slot = pos_ref[i * tm + r]
        return pltpu.make_async_copy(ys_ref.at[pl.ds(slot, 1), :], ybuf_ref.at[pl.ds(r, 1), :], sem)

    def start(r, carry):
        row_copy(r).start()
        return carry

    def wait(r, carry):
        row_copy(r).wait()
        return carry

    lax.fori_loop(0, tm, start, 0, unroll=8)
    lax.fori_loop(0, tm, wait, 0, unroll=8)
    o_ref[...] = _layer_norm(ALPHA * x_ref[...] + ybuf_ref[...], g_ref[...], b_ref[...])


def _gather_ln(pos, x1, ys, gain, bias):
    t_tokens, d = x1.shape
    tm = min(512, t_tokens)
    grid_spec = pltpu.PrefetchScalarGridSpec(
        num_scalar_prefetch=1,
        grid=(t_tokens // tm,),
        in_specs=[pl.BlockSpec((tm, d), lambda i, pos: (i, 0)),
                  pl.BlockSpec(memory_space=pl.ANY),
                  pl.BlockSpec((1, d), lambda i, pos: (0, 0)),
                  pl.BlockSpec((1, d), lambda i, pos: (0, 0))],
        out_specs=pl.BlockSpec((tm, d), lambda i, pos: (i, 0)),
        scratch_shapes=[pltpu.VMEM((tm, d), F32), pltpu.SemaphoreType.DMA],
    )
    return pl.pallas_call(
        _gather_ln_kernel,
        out_shape=jax.ShapeDtypeStruct((t_tokens, d), F32),
        grid_spec=grid_spec,
        compiler_params=_cparams("arbitrary"),
        name="moe_gather_ln",
    )(pos, x1, ys, gain, bias)


def _moe_ln(x1, route, w_gate, w_up, w_down, gain, bias):
    t_tokens = x1.shape[0]
    n_tiles = t_tokens // MOE_TILE + N_BUCKETS
    rank, cnt = _bucket_ranks(route)
    pos, ends, padded, tile_ea, tile_eb, tile_src, n_used = _sort_plan(route, rank, cnt, n_tiles)
    xs = _scatter_rows(pos, ends, padded, x1, route, n_tiles * MOE_TILE)
    ys = _run_experts(tile_ea, tile_eb, tile_src, n_used, xs, w_gate, w_up, w_down)
    return _gather_ln(pos, x1, ys, gain, bias)


def _pad_lanes(w):
    return jnp.pad(w, ((0, 0), (0, LANES - w.shape[1])))


def kernel(x, ln_gain, ln_bias, swa_w_in, swa_sinks, swa_w_o, fox_w_in, fox_b_f, fox_w_o,
           w_router, b_router, w_gate, w_up, w_down):
    batch, seq, d = x.shape
    t_tokens = batch * seq
    tables = _rope_tables(seq)
    wr = jnp.concatenate(_split_bf16(_pad_lanes(w_router)), axis=1)
    br = _pad_lanes(b_router[None, :])
    fox_qkv = 3 * FOX_HEADS * HEAD_DIM
    x2 = x.reshape(t_tokens, d)
    for i in range(DEPTH):
        j = i // 2
        if i % 2 == 0:
            proj = _in_projection(x2, swa_w_in[j].astype(BF16), seq,
                                  q_cols=SWA_Q_HEADS * HEAD_DIM, q_scale=QK_SCALE,
                                  rope_cols=(SWA_Q_HEADS + SWA_KV_HEADS) * HEAD_DIM,
                                  tables=tables)
            attn = _swa_attention(proj.reshape(batch, seq, -1), swa_sinks[j], batch, seq)
            w_o = swa_w_o[j]
        else:
            w_in = fox_w_in[j]
            proj = _in_projection(x2, w_in[:, :fox_qkv].astype(BF16), seq,
                                  q_cols=FOX_HEADS * HEAD_DIM, q_scale=QK_SCALE * LOG2E,
                                  rope_cols=0, tables=tables)
            ccol = _decay_cumsum(x2.reshape(batch, seq, d),
                                 _pad_lanes(w_in[:, fox_qkv:]).astype(BF16),
                                 _pad_lanes(fox_b_f[j][None, :]))
            attn = _fox_attention(proj.reshape(batch, seq, -1), ccol, batch, seq)
            w_o = fox_w_o[j]
        x1, route = _outproj_ln_router(attn.reshape(t_tokens, d), w_o.astype(BF16), x2,
                                       ln_gain[i, 0][None, :], ln_bias[i, 0][None, :],
                                       wr, br)
        x2 = _moe_ln(x1, route, w_gate[i], w_up[i], w_down[i],
                     ln_gain[i, 1][None, :], ln_bias[i, 1][None, :])
    return x2.reshape(batch, seq, d)
```

```python
import functools
import math

import jax
import jax.numpy as jnp
from jax import lax
from jax.experimental import pallas as pl
from jax.experimental.pallas import tpu as pltpu

F32 = jnp.float32
BF16 = jnp.bfloat16

HEAD_DIM = 64
ROT_DIM = 16
ROT_HALF = ROT_DIM // 2
ROPE_THETA = 500000.0
SWA_Q_HEADS = 16
SWA_KV_HEADS = 4
SWA_BLOCK = 128
FOX_HEADS = 16
N_EXPERTS = 16
N_GROUPS = 4
EXPERTS_PER_GROUP = 4
D_EXPERT = 256
DEPTH = 4
ALPHA = (2.0 * DEPTH) ** 0.25
LN_EPS = 1e-5
NEG = -1e30
QK_SCALE = HEAD_DIM ** -0.5
LOG2E = math.log2(math.e)

LANES = 128
VMEM_LIMIT = 56 * 1024 * 1024


def _cparams(*sem):
    return pltpu.CompilerParams(dimension_semantics=sem, vmem_limit_bytes=VMEM_LIMIT)


def _inproj_kernel(x_ref, w_ref, cos_ref, sa_ref, sb_ref, o_ref, *, tn, q_cols, q_scale,
                   rope_cols):
    xb = x_ref[...].astype(BF16)
    n = o_ref.shape[1]
    for c0 in range(0, n, tn):
        acc = jnp.dot(xb, w_ref[:, c0:c0 + tn].astype(BF16), preferred_element_type=F32)
        n_rope = min(max(rope_cols - c0, 0), tn)
        if n_rope:
            reps = n_rope // LANES
            head = acc[:, :n_rope]
            head = (head * jnp.tile(cos_ref[...], (1, reps))
                    + pltpu.roll(head, ROT_HALF, 1) * jnp.tile(sa_ref[...], (1, reps))
                    + pltpu.roll(head, n_rope - ROT_HALF, 1) * jnp.tile(sb_ref[...], (1, reps)))
            acc = head if n_rope == tn else jnp.concatenate([head, acc[:, n_rope:]], axis=1)
        if c0 + tn <= q_cols:
            acc = acc * q_scale
        else:
            assert c0 >= q_cols, "q columns must end on a chunk boundary"
        o_ref[:, c0:c0 + tn] = acc.astype(o_ref.dtype)


def _rope_tables(seq):
    inv_freq = jnp.power(ROPE_THETA, -jnp.arange(ROT_HALF, dtype=F32) * (2.0 / ROT_DIM))
    ang = jnp.arange(seq, dtype=F32)[:, None] * inv_freq[None, :]
    cos, sin = jnp.cos(ang), jnp.sin(ang)
    ones = jnp.ones((seq, HEAD_DIM - ROT_DIM), F32)
    zeros = jnp.zeros((seq, HEAD_DIM - ROT_DIM), F32)
    z8 = jnp.zeros((seq, ROT_HALF), F32)
    cos_h = jnp.concatenate([cos, cos, ones], axis=1)
    sa_h = jnp.concatenate([z8, sin, zeros], axis=1)
    sb_h = jnp.concatenate([-sin, z8, zeros], axis=1)
    rep = LANES // HEAD_DIM
    return tuple(jnp.tile(t, (1, rep)) for t in (cos_h, sa_h, sb_h))


def _in_projection(x2, w, n, seq, *, q_cols, q_scale, rope_cols, tables):
    t_tokens, d = x2.shape
    tm = min(1024, seq)
    tn = 512
    assert t_tokens % tm == 0 and seq % tm == 0 and n % tn == 0
    pos_blocks = seq // tm
    kern = functools.partial(_inproj_kernel, tn=tn, q_cols=q_cols, q_scale=q_scale,
                             rope_cols=rope_cols)
    tab_spec = pl.BlockSpec((tm, LANES), lambda i: (i % pos_blocks, 0))
    return pl.pallas_call(
        kern,
        out_shape=jax.ShapeDtypeStruct((t_tokens, n), BF16),
        grid=(t_tokens // tm,),
        in_specs=[pl.BlockSpec((tm, d), lambda i: (i, 0)),
                  pl.BlockSpec(w.shape, lambda i: (0, 0), pipeline_mode=pl.Buffered(1)),
                  tab_spec, tab_spec, tab_spec],
        out_specs=pl.BlockSpec((tm, n), lambda i: (i, 0)),
        compiler_params=_cparams("parallel"),
        name="in_projection",
    )(x2, w, *tables)


def _swa_kernel(sink_ref, q_ref, kp_ref, kc_ref, vp_ref, vc_ref, o_ref):
    n = pl.program_id(1)
    blk = SWA_BLOCK
    lane = lax.broadcasted_iota(jnp.int32, (1, LANES), 1)
    lo = lane < HEAD_DIM
    r = lax.broadcasted_iota(jnp.int32, (blk, 2 * blk), 0)
    c = lax.broadcasted_iota(jnp.int32, (blk, 2 * blk), 1)
    allowed = (c > r) & (c <= r + blk) & ((c >= blk) | (n > 0))
    k_all = jnp.concatenate([kp_ref[0], kc_ref[0]], axis=0).astype(F32)
    v_all = jnp.concatenate([vp_ref[0], vc_ref[0]], axis=0).astype(F32)
    for g in range(SWA_KV_HEADS):
        grp = g // 2
        kf = k_all[:, grp * LANES:(grp + 1) * LANES]
        vf = v_all[:, grp * LANES:(grp + 1) * LANES]
        kr = pltpu.roll(kf, HEAD_DIM, 1)
        vr = pltpu.roll(vf, HEAD_DIM, 1)
        own_lo = (g % 2 == 0)
        k_src_lo, k_src_hi = (kf, kr) if own_lo else (kr, kf)
        k_lo = jnp.where(lo, k_src_lo, 0.0).astype(BF16)
        k_hi = jnp.where(lo, 0.0, k_src_hi).astype(BF16)
        v_dup = jnp.where(lo, vf if own_lo else vr, vr if own_lo else vf).astype(BF16)
        for pp in range(2):
            p = 2 * g + pp
            qp = q_ref[0, :, p * LANES:(p + 1) * LANES]
            outs = []
            for e, kk in enumerate((k_lo, k_hi)):
                s = lax.dot_general(qp, kk, (((1,), (1,)), ((), ())),
                                    preferred_element_type=F32)
                s = jnp.where(allowed, s, NEG)
                sink = sink_ref[2 * p + e]
                m = jnp.maximum(jnp.max(s, axis=1, keepdims=True), sink)
                pr = jnp.exp(s - m)
                den = jnp.sum(pr, axis=1, keepdims=True) + jnp.exp(sink - m)
                o = jnp.dot(pr.astype(BF16), v_dup, preferred_element_type=F32)
                outs.append(o / den)
            o_ref[0, :, p * LANES:(p + 1) * LANES] = jnp.where(lo, outs[0], outs[1]).astype(o_ref.dtype)


def _swa_attention(proj, sinks, batch, seq):
    blk = SWA_BLOCK
    nb = seq // blk
    qd = SWA_Q_HEADS * HEAD_DIM
    kd = SWA_KV_HEADS * HEAD_DIM
    kcol, vcol = qd // kd, qd // kd + 1
    prev = lambda b, n: jnp.maximum(n - 1, 0)
    return pl.pallas_call(
        _swa_kernel,
        out_shape=jax.ShapeDtypeStruct((batch, seq, qd), BF16),
        grid=(batch, nb),
        in_specs=[pl.BlockSpec(memory_space=pltpu.SMEM),
                  pl.BlockSpec((1, blk, qd), lambda b, n: (b, n, 0)),
                  pl.BlockSpec((1, blk, kd), lambda b, n: (b, prev(b, n), kcol)),
                  pl.BlockSpec((1, blk, kd), lambda b, n: (b, n, kcol)),
                  pl.BlockSpec((1, blk, kd), lambda b, n: (b, prev(b, n), vcol)),
                  pl.BlockSpec((1, blk, kd), lambda b, n: (b, n, vcol))],
        out_specs=pl.BlockSpec((1, blk, qd), lambda b, n: (b, n, 0)),
        compiler_params=_cparams("parallel", "arbitrary"),
        name="swa_attention",
    )(sinks, proj, proj, proj, proj, proj)


def _decay_kernel(x_ref, wf_ref, bf_ref, ccol_ref):
    seq = x_ref.shape[1]
    f = jnp.dot(x_ref[0].astype(BF16), wf_ref[...], preferred_element_type=F32) + bf_ref[...]
    ls = jnp.minimum(f, 0.0) - jnp.log1p(jnp.exp(-jnp.abs(f)))
    row = lax.broadcasted_iota(jnp.int32, ls.shape, 0)
    c = ls
    k = 1
    while k < seq:
        c = c + jnp.where(row >= k, pltpu.roll(c, k, 0), 0.0)
        k *= 2
    ccol_ref[0] = c * LOG2E


def _decay_cumsum(x3, wf, bfv):
    batch, seq, d = x3.shape
    return pl.pallas_call(
        _decay_kernel,
        out_shape=jax.ShapeDtypeStruct((batch, seq, LANES), F32),
        grid=(batch,),
        in_specs=[pl.BlockSpec((1, seq, d), lambda b: (b, 0, 0)),
                  pl.BlockSpec((d, LANES), lambda b: (0, 0)),
                  pl.BlockSpec((1, LANES), lambda b: (0, 0))],
        out_specs=pl.BlockSpec((1, seq, LANES), lambda b: (b, 0, 0)),
        compiler_params=_cparams("parallel"),
        name="fox_decay_cumsum",
    )(x3, wf, bfv)


def _split3_bf16(v):
    hi = v.astype(BF16).astype(F32)
    r = v - hi
    mid = r.astype(BF16).astype(F32)
    lo = (r - mid).astype(BF16).astype(F32)
    return hi, mid, lo


def _fox_kernel(q_ref, k_ref, v_ref, ccol_ref, o_ref, spec_ref, *, tq, nq, pps):
    hp = pl.program_id(1)
    qi = pl.program_id(2)
    n_heads = 2 * pps
    ones0 = 3 * n_heads
    lane = lax.broadcasted_iota(jnp.int32, (1, LANES), 1)
    lo_half = lane < HEAD_DIM
    heads = [(pr, e) for pr in range(pps) for e in range(2)]

    @pl.when(qi == 0)
    def _():
        parts = jnp.concatenate(_split3_bf16(-ccol_ref[0]), axis=1).astype(BF16)
        r = lax.broadcasted_iota(jnp.int32, (3 * LANES, LANES), 0)
        c = lax.broadcasted_iota(jnp.int32, (3 * LANES, LANES), 1)
        head_i = r % LANES - hp * n_heads
        pick = (head_i >= 0) & (head_i < n_heads) & (c == 3 * head_i + r // LANES)
        sel = jnp.where(pick, 1.0, 0.0).astype(BF16)
        spec = jnp.dot(parts, sel, preferred_element_type=F32)
        spec = jnp.where((lane >= ones0) & (lane < ones0 + 3), 1.0, spec)
        spec_ref[...] = spec.astype(BF16)

    ct_all = ccol_ref[0, pl.ds(pl.multiple_of(qi * tq, tq), tq), :]
    lane_c = lax.broadcasted_iota(jnp.int32, ct_all.shape, 1)
    q_heads = []
    for i, (pr, e) in enumerate(heads):
        qp = q_ref[0, :, pr * LANES:(pr + 1) * LANES]
        own = lo_half if e == 0 else jnp.logical_not(lo_half)
        qa = jnp.where(own, qp, jnp.zeros_like(qp))
        ct = jnp.sum(jnp.where(lane_c == hp * n_heads + i, ct_all, 0.0), axis=1, keepdims=True)
        hi, mid, lo = _split3_bf16(ct)
        qs = jnp.where((lane >= 3 * i) & (lane < 3 * i + 3), 1.0, 0.0)
        qs = jnp.where(lane == ones0, hi, qs)
        qs = jnp.where(lane == ones0 + 1, mid, qs)
        qs = jnp.where(lane == ones0 + 2, lo, qs)
        q_heads.append(jnp.concatenate([qa, qs.astype(BF16)], axis=1))
    rr = lax.broadcasted_iota(jnp.int32, (tq, tq), 0)
    cc_i = lax.broadcasted_iota(jnp.int32, (tq, tq), 1)
    causal = rr >= cc_i
    nchunk = tq // LANES

    def chunks(a):
        return [a[:, i * LANES:(i + 1) * LANES] for i in range(nchunk)]

    def attend(n):
        outs = []
        for idx, (pr, e) in enumerate(heads):
            s_blocks = []
            for j in range(n + 1):
                rows = slice(j * tq, (j + 1) * tq)
                kj = jnp.concatenate([k_ref[0, rows, pr * LANES:(pr + 1) * LANES],
                                      spec_ref[rows, :]], axis=1)
                s = lax.dot_general(q_heads[idx], kj, (((1,), (1,)), ((), ())),
                                    preferred_element_type=F32)
                if j == n:
                    s = jnp.where(causal, s, NEG)
                s_blocks.append(s)
            m_part = functools.reduce(jnp.maximum, [ch for s in s_blocks for ch in chunks(s)])
            m = jnp.max(m_part, axis=1, keepdims=True)
            l_part = jnp.zeros((tq, LANES), F32)
            o = jnp.zeros((tq, LANES), F32)
            for j in range(n + 1):
                p = jnp.exp2(s_blocks[j] - m)
                l_part = l_part + functools.reduce(lambda a, b: a + b, chunks(p))
                vj = v_ref[0, j * tq:(j + 1) * tq, pr * LANES:(pr + 1) * LANES]
                o = o + jnp.dot(p.astype(BF16), vj, preferred_element_type=F32)
            outs.append(o / jnp.sum(l_part, axis=1, keepdims=True))
        for pr in range(pps):
            o_ref[0, :, pr * LANES:(pr + 1) * LANES] = jnp.where(
                lo_half, outs[2 * pr], outs[2 * pr + 1]).astype(o_ref.dtype)

    lax.switch(qi, [functools.partial(attend, n) for n in range(nq)])


FOX_PAIRS_PER_STEP = 4


def _fox_attention(proj, ccol, batch, seq):
    tq = min(256, seq)
    nq = seq // tq
    hd_all = FOX_HEADS * HEAD_DIM
    pps = FOX_PAIRS_PER_STEP
    width = pps * LANES
    groups = hd_all // width
    kern = functools.partial(_fox_kernel, tq=tq, nq=nq, pps=pps)
    return pl.pallas_call(
        kern,
        out_shape=jax.ShapeDtypeStruct((batch, seq, hd_all), BF16),
        grid=(batch, groups, nq),
        in_specs=[pl.BlockSpec((1, tq, width), lambda b, h, i: (b, i, h)),
                  pl.BlockSpec((1, seq, width), lambda b, h, i: (b, 0, groups + h)),
                  pl.BlockSpec((1, seq, width), lambda b, h, i: (b, 0, 2 * groups + h)),
                  pl.BlockSpec((1, seq, LANES), lambda b, h, i: (b, 0, 0))],
        out_specs=pl.BlockSpec((1, tq, width), lambda b, h, i: (b, i, h)),
        scratch_shapes=[pltpu.VMEM((seq, LANES), BF16)],
        compiler_params=_cparams("parallel", "parallel", "arbitrary"),
        name="fox_attention",
    )(proj, proj, proj, ccol)


def _layer_norm(y, gain, bias):
    mu = jnp.mean(y, axis=-1, keepdims=True)
    d = y - mu
    var = jnp.mean(d * d, axis=-1, keepdims=True)
    return d * lax.rsqrt(var + LN_EPS) * gain + bias


def _split_bf16(v):
    hi = v.astype(BF16)
    lo = (v - hi.astype(F32)).astype(BF16)
    return hi, lo


def _route(logits_t):
    rows = [logits_t[e:e + 1, :] for e in range(N_EXPERTS)]
    mx = functools.reduce(jnp.maximum, rows)
    ex = [jnp.exp(v - mx) for v in rows]
    den = functools.reduce(lambda a, b: a + b, ex)
    sc = [v / den for v in ex]
    gscore = []
    for g in range(N_GROUPS):
        mem = sc[g * EXPERTS_PER_GROUP:(g + 1) * EXPERTS_PER_GROUP]
        pairs = [mem[i] + mem[j] for i in range(4) for j in range(i + 1, 4)]
        gscore.append(functools.reduce(jnp.maximum, pairs))
    gmax = functools.reduce(jnp.maximum, gscore)
    taken = None
    gates = []
    for g in range(N_GROUPS):
        eq = gscore[g] == gmax
        best = eq if taken is None else eq & jnp.logical_not(taken)
        taken = eq if taken is None else taken | eq
        mem = sc[g * EXPERTS_PER_GROUP:(g + 1) * EXPERTS_PER_GROUP]
        sel = []
        for i in range(4):
            rank = jnp.zeros_like(mem[i])
            for j in range(4):
                if j == i:
                    continue
                ahead = (mem[j] > mem[i]) | ((mem[j] == mem[i]) & (j < i))
                rank = rank + jnp.where(ahead, 1.0, 0.0)
            sel.append(best & (rank < 2.0))
        tot = functools.reduce(lambda a, b: a + b,
                               [jnp.where(sel[i], mem[i], 0.0) for i in range(4)])
        for i in range(4):
            gates.append(jnp.where(sel[i], mem[i] / tot, 0.0))
    return gates


OUTPROJ_SUBTILES = 2


def _outproj_kernel(a_ref, wo_ref, x_ref, g_ref, b_ref, wr_ref, br_ref, x1_ref, gate_ref):
    sub = x_ref.shape[0] // OUTPROJ_SUBTILES
    wo = wo_ref[...].astype(BF16)
    for h in range(OUTPROJ_SUBTILES):
        rows = slice(h * sub, (h + 1) * sub)
        mix = jnp.dot(a_ref[rows, :], wo, preferred_element_type=F32)
        x1 = _layer_norm(ALPHA * x_ref[rows, :] + mix, g_ref[...], b_ref[...])
        x1_ref[rows, :] = x1
        hi, lo = _split_bf16(x1)
        from_hi = jnp.dot(hi, wr_ref[...], preferred_element_type=F32)
        from_lo = jnp.dot(lo, wr_ref[...], preferred_element_type=F32)
        logits = (from_hi[:, :LANES] + (from_lo[:, :LANES] + from_hi[:, LANES:])) + br_ref[...]
        routed = _route(logits.T[:N_EXPERTS, :])
        pad = jnp.zeros((LANES - len(routed), sub), F32)
        gate_ref[rows, :] = jnp.concatenate(list(routed) + [pad], axis=0).T


def _outproj_ln_router(a2, wo, x2, gain, bias, wr, br):
    t_tokens, d = x2.shape
    tm = min(512, t_tokens)
    row = lambda i: (i, 0)
    fixed = lambda i: (0, 0)
    return pl.pallas_call(
        _outproj_kernel,
        out_shape=(jax.ShapeDtypeStruct((t_tokens, d), F32),
                   jax.ShapeDtypeStruct((t_tokens, LANES), F32)),
        grid=(t_tokens // tm,),
        in_specs=[pl.BlockSpec((tm, d), row),
                  pl.BlockSpec((d, d), fixed),
                  pl.BlockSpec((tm, d), row),
                  pl.BlockSpec((1, d), fixed),
                  pl.BlockSpec((1, d), fixed),
                  pl.BlockSpec((d, 2 * LANES), fixed),
                  pl.BlockSpec((1, LANES), fixed)],
        out_specs=(pl.BlockSpec((tm, d), row), pl.BlockSpec((tm, LANES), row)),
        compiler_params=_cparams("parallel"),
        name="outproj_ln_router",
    )(a2, wo, x2, gain, bias, wr, br)


def _moe_kernel(x_ref, gate_ref, wg_ref, wu_ref, wd_ref, g_ref, b_ref, o_ref, xb_ref, acc_ref):
    grp = pl.program_id(1)

    @pl.when(grp == 0)
    def _():
        xb_ref[...] = x_ref[...].astype(BF16)
        acc_ref[...] = jnp.zeros_like(acc_ref)

    gates = gate_ref[...]
    lane = lax.broadcasted_iota(jnp.int32, gates.shape, 1)
    for k in range(EXPERTS_PER_GROUP):
        xb = xb_ref[...]
        hg = jnp.dot(xb, wg_ref[k].astype(BF16), preferred_element_type=F32)
        hu = jnp.dot(xb, wu_ref[k].astype(BF16), preferred_element_type=F32)
        ge = jnp.sum(jnp.where(lane == grp * EXPERTS_PER_GROUP + k, gates, 0.0),
                     axis=1, keepdims=True)
        act = hg * jax.nn.sigmoid(hg) * hu * ge
        acc_ref[...] += jnp.dot(act.astype(BF16), wd_ref[k].astype(BF16),
                                preferred_element_type=F32)

    @pl.when(grp == N_GROUPS - 1)
    def _():
        o_ref[...] = _layer_norm(ALPHA * x_ref[...] + acc_ref[...], g_ref[...], b_ref[...])


def _moe_ln(x2, gates, w_gate, w_up, w_down, gain, bias):
    t_tokens, d = x2.shape
    tm = min(1024, t_tokens)
    row = lambda i, e: (i, 0)
    fixed = lambda i, e: (0, 0)
    per_group = lambda i, e: (e, 0, 0)
    return pl.pallas_call(
        _moe_kernel,
        out_shape=jax.ShapeDtypeStruct((t_tokens, d), F32),
        grid=(t_tokens // tm, N_GROUPS),
        in_specs=[pl.BlockSpec((tm, d), row),
                  pl.BlockSpec((tm, LANES), row),
                  pl.BlockSpec((EXPERTS_PER_GROUP, d, D_EXPERT), per_group),
                  pl.BlockSpec((EXPERTS_PER_GROUP, d, D_EXPERT), per_group),
                  pl.BlockSpec((EXPERTS_PER_GROUP, D_EXPERT, d), per_group),
                  pl.BlockSpec((1, d), fixed),
                  pl.BlockSpec((1, d), fixed)],
        out_specs=pl.BlockSpec((tm, d), row),
        scratch_shapes=[pltpu.VMEM((tm, d), BF16), pltpu.VMEM((tm, d), F32)],
        compiler_params=_cparams("parallel", "arbitrary"),
        name="moe_ln",
    )(x2, gates, w_gate, w_up, w_down, gain, bias)


def _pad_lanes(w):
    return jnp.pad(w, ((0, 0), (0, LANES - w.shape[1])))


def kernel(x, ln_gain, ln_bias, swa_w_in, swa_sinks, swa_w_o, fox_w_in, fox_b_f, fox_w_o,
           w_router, b_router, w_gate, w_up, w_down):
    batch, seq, d = x.shape
    t_tokens = batch * seq
    tables = _rope_tables(seq)
    wr = jnp.concatenate(_split_bf16(_pad_lanes(w_router)), axis=1)
    br = _pad_lanes(b_router[None, :])
    fox_qkv = 3 * FOX_HEADS * HEAD_DIM
    x2 = x.reshape(t_tokens, d)
    for i in range(DEPTH):
        j = i // 2
        if i % 2 == 0:
            proj = _in_projection(x2, swa_w_in[j], swa_w_in.shape[2], seq,
                                  q_cols=SWA_Q_HEADS * HEAD_DIM, q_scale=QK_SCALE,
                                  rope_cols=(SWA_Q_HEADS + SWA_KV_HEADS) * HEAD_DIM,
                                  tables=tables)
            attn = _swa_attention(proj.reshape(batch, seq, -1), swa_sinks[j], batch, seq)
            w_o = swa_w_o[j]
        else:
            w_in = fox_w_in[j]
            proj = _in_projection(x2, w_in, fox_qkv, seq,
                                  q_cols=FOX_HEADS * HEAD_DIM, q_scale=QK_SCALE * LOG2E,
                                  rope_cols=0, tables=tables)
            ccol = _decay_cumsum(x2.reshape(batch, seq, d),
                                 _pad_lanes(w_in[:, fox_qkv:]).astype(BF16),
                                 _pad_lanes(fox_b_f[j][None, :]))
            attn = _fox_attention(proj.reshape(batch, seq, -1), ccol, batch, seq)
            w_o = fox_w_o[j]
        x1, gates = _outproj_ln_router(attn.reshape(t_tokens, d), w_o, x2,
                                       ln_gain[i, 0][None, :], ln_bias[i, 0][None, :],
                                       wr, br)
        x2 = _moe_ln(x1, gates, w_gate[i], w_up[i], w_down[i],
                     ln_gain[i, 1][None, :], ln_bias[i, 1][None, :])
    return x2.reshape(batch, seq, d)
```

```python
import functools
import math

import jax
import jax.numpy as jnp
from jax import lax
from jax.experimental import pallas as pl
from jax.experimental.pallas import tpu as pltpu

F32 = jnp.float32
BF16 = jnp.bfloat16

HEAD_DIM = 64
ROT_DIM = 16
ROT_HALF = ROT_DIM // 2
ROPE_THETA = 500000.0
SWA_Q_HEADS = 16
SWA_KV_HEADS = 4
SWA_BLOCK = 128
FOX_HEADS = 16
N_EXPERTS = 16
N_GROUPS = 4
EXPERTS_PER_GROUP = 4
D_EXPERT = 256
DEPTH = 4
ALPHA = (2.0 * DEPTH) ** 0.25
LN_EPS = 1e-5
NEG = -1e30
QK_SCALE = HEAD_DIM ** -0.5
LOG2E = math.log2(math.e)

LANES = 128
VMEM_LIMIT = 56 * 1024 * 1024


def _cparams(*sem):
    return pltpu.CompilerParams(dimension_semantics=sem, vmem_limit_bytes=VMEM_LIMIT)


def _inproj_kernel(x_ref, w_ref, cos_ref, sa_ref, sb_ref, o_ref, *, tn, q_cols, q_scale,
                   rope_cols):
    xb = x_ref[...].astype(BF16)
    n = o_ref.shape[1]
    for c0 in range(0, n, tn):
        acc = jnp.dot(xb, w_ref[:, c0:c0 + tn].astype(BF16), preferred_element_type=F32)
        n_rope = min(max(rope_cols - c0, 0), tn)
        if n_rope:
            reps = n_rope // LANES
            head = acc[:, :n_rope]
            head = (head * jnp.tile(cos_ref[...], (1, reps))
                    + pltpu.roll(head, ROT_HALF, 1) * jnp.tile(sa_ref[...], (1, reps))
                    + pltpu.roll(head, n_rope - ROT_HALF, 1) * jnp.tile(sb_ref[...], (1, reps)))
            acc = head if n_rope == tn else jnp.concatenate([head, acc[:, n_rope:]], axis=1)
        if c0 + tn <= q_cols:
            acc = acc * q_scale
        else:
            assert c0 >= q_cols, "q columns must end on a chunk boundary"
        o_ref[:, c0:c0 + tn] = acc.astype(o_ref.dtype)


def _rope_tables(seq):
    inv_freq = jnp.power(ROPE_THETA, -jnp.arange(ROT_HALF, dtype=F32) * (2.0 / ROT_DIM))
    ang = jnp.arange(seq, dtype=F32)[:, None] * inv_freq[None, :]
    cos, sin = jnp.cos(ang), jnp.sin(ang)
    ones = jnp.ones((seq, HEAD_DIM - ROT_DIM), F32)
    zeros = jnp.zeros((seq, HEAD_DIM - ROT_DIM), F32)
    z8 = jnp.zeros((seq, ROT_HALF), F32)
    cos_h = jnp.concatenate([cos, cos, ones], axis=1)
    sa_h = jnp.concatenate([z8, sin, zeros], axis=1)
    sb_h = jnp.concatenate([-sin, z8, zeros], axis=1)
    rep = LANES // HEAD_DIM
    return tuple(jnp.tile(t, (1, rep)) for t in (cos_h, sa_h, sb_h))


def _in_projection(x2, w_stack, layer, n, seq, *, q_cols, q_scale, rope_cols, tables):
    t_tokens, d = x2.shape
    tm = min(1024, seq)
    tn = 512
    assert t_tokens % tm == 0 and seq % tm == 0 and n % tn == 0
    pos_blocks = seq // tm
    kern = functools.partial(_inproj_kernel, tn=tn, q_cols=q_cols, q_scale=q_scale,
                             rope_cols=rope_cols)
    tab_spec = pl.BlockSpec((tm, LANES), lambda i: (i % pos_blocks, 0))
    return pl.pallas_call(
        kern,
        out_shape=jax.ShapeDtypeStruct((t_tokens, n), BF16),
        grid=(t_tokens // tm,),
        in_specs=[pl.BlockSpec((tm, d), lambda i: (i, 0)),
                  pl.BlockSpec((None,) + w_stack.shape[1:], lambda i: (layer, 0, 0),
                               pipeline_mode=pl.Buffered(1)),
                  tab_spec, tab_spec, tab_spec],
        out_specs=pl.BlockSpec((tm, n), lambda i: (i, 0)),
        compiler_params=_cparams("parallel"),
        name="in_projection",
    )(x2, w_stack, *tables)


def _swa_kernel(sink_ref, q_ref, kp_ref, kc_ref, vp_ref, vc_ref, o_ref):
    n = pl.program_id(1)
    blk = SWA_BLOCK
    lane = lax.broadcasted_iota(jnp.int32, (1, LANES), 1)
    lo = lane < HEAD_DIM
    r = lax.broadcasted_iota(jnp.int32, (blk, 2 * blk), 0)
    c = lax.broadcasted_iota(jnp.int32, (blk, 2 * blk), 1)
    allowed = (c > r) & (c <= r + blk) & ((c >= blk) | (n > 0))
    k_all = jnp.concatenate([kp_ref[0], kc_ref[0]], axis=0).astype(F32)
    v_all = jnp.concatenate([vp_ref[0], vc_ref[0]], axis=0).astype(F32)
    for g in range(SWA_KV_HEADS):
        grp = g // 2
        kf = k_all[:, grp * LANES:(grp + 1) * LANES]
        vf = v_all[:, grp * LANES:(grp + 1) * LANES]
        kr = pltpu.roll(kf, HEAD_DIM, 1)
        vr = pltpu.roll(vf, HEAD_DIM, 1)
        own_lo = (g % 2 == 0)
        k_src_lo, k_src_hi = (kf, kr) if own_lo else (kr, kf)
        k_lo = jnp.where(lo, k_src_lo, 0.0).astype(BF16)
        k_hi = jnp.where(lo, 0.0, k_src_hi).astype(BF16)
        v_lo = jnp.where(lo, vf if own_lo else vr, 1.0).astype(BF16)
        v_hi = jnp.where(lo, 1.0, vr if own_lo else vf).astype(BF16)
        for pp in range(2):
            p = 2 * g + pp
            qp = q_ref[0, :, p * LANES:(p + 1) * LANES]
            outs, sink_terms = [], []
            for e, (kk, vv) in enumerate(((k_lo, v_lo), (k_hi, v_hi))):
                s = lax.dot_general(qp, kk, (((1,), (1,)), ((), ())),
                                    preferred_element_type=F32)
                s = jnp.where(allowed, s, NEG)
                sink = sink_ref[2 * p + e]
                m = jnp.maximum(jnp.max(s, axis=1, keepdims=True), sink)
                pr = jnp.exp2(s - m)
                outs.append(jnp.dot(pr.astype(BF16), vv, preferred_element_type=F32))
                sink_terms.append(jnp.exp2(sink - m))
            num = jnp.where(lo, outs[0], outs[1])
            den = (pltpu.roll(jnp.where(lo, outs[1], outs[0]), HEAD_DIM, 1)
                   + jnp.where(lo, sink_terms[0], sink_terms[1]))
            o_ref[0, :, p * LANES:(p + 1) * LANES] = (num / den).astype(o_ref.dtype)


def _swa_attention(proj, sinks, batch, seq):
    blk = SWA_BLOCK
    nb = seq // blk
    qd = SWA_Q_HEADS * HEAD_DIM
    kd = SWA_KV_HEADS * HEAD_DIM
    kcol, vcol = qd // kd, qd // kd + 1
    prev = lambda b, n: jnp.maximum(n - 1, 0)
    return pl.pallas_call(
        _swa_kernel,
        out_shape=jax.ShapeDtypeStruct((batch, seq, qd), BF16),
        grid=(batch, nb),
        in_specs=[pl.BlockSpec(memory_space=pltpu.SMEM),
                  pl.BlockSpec((1, blk, qd), lambda b, n: (b, n, 0)),
                  pl.BlockSpec((1, blk, kd), lambda b, n: (b, prev(b, n), kcol)),
                  pl.BlockSpec((1, blk, kd), lambda b, n: (b, n, kcol)),
                  pl.BlockSpec((1, blk, kd), lambda b, n: (b, prev(b, n), vcol)),
                  pl.BlockSpec((1, blk, kd), lambda b, n: (b, n, vcol))],
        out_specs=pl.BlockSpec((1, blk, qd), lambda b, n: (b, n, 0)),
        compiler_params=_cparams("parallel", "arbitrary"),
        name="swa_attention",
    )(sinks, proj, proj, proj, proj, proj)


def _decay_kernel(x_ref, wf_ref, bf_ref, ccol_ref):
    seq = x_ref.shape[1]
    f = jnp.dot(x_ref[0].astype(BF16), wf_ref[...], preferred_element_type=F32) + bf_ref[...]
    ls = jnp.minimum(f, 0.0) - jnp.log1p(jnp.exp(-jnp.abs(f)))
    row = lax.broadcasted_iota(jnp.int32, ls.shape, 0)
    c = ls
    k = 1
    while k < seq:
        c = c + jnp.where(row >= k, pltpu.roll(c, k, 0), 0.0)
        k *= 2
    ccol_ref[0] = c * LOG2E


def _decay_cumsum(x3, wf, bfv):
    batch, seq, d = x3.shape
    return pl.pallas_call(
        _decay_kernel,
        out_shape=jax.ShapeDtypeStruct((batch, seq, LANES), F32),
        grid=(batch,),
        in_specs=[pl.BlockSpec((1, seq, d), lambda b: (b, 0, 0)),
                  pl.BlockSpec((d, LANES), lambda b: (0, 0)),
                  pl.BlockSpec((1, LANES), lambda b: (0, 0))],
        out_specs=pl.BlockSpec((1, seq, LANES), lambda b: (b, 0, 0)),
        compiler_params=_cparams("parallel"),
        name="fox_decay_cumsum",
    )(x3, wf, bfv)


def _split3_bf16(v):
    hi = v.astype(BF16).astype(F32)
    r = v - hi
    mid = r.astype(BF16).astype(F32)
    lo = (r - mid).astype(BF16).astype(F32)
    return hi, mid, lo


def _fox_kernel(q_ref, k_ref, v_ref, ccol_ref, o_ref, spec_ref, *, tq, nq, pps):
    hp = pl.program_id(1)
    qi = pl.program_id(2)
    n_heads = 2 * pps
    ones0 = 3 * n_heads
    lane = lax.broadcasted_iota(jnp.int32, (1, LANES), 1)
    lo_half = lane < HEAD_DIM
    heads = [(pr, e) for pr in range(pps) for e in range(2)]

    @pl.when(qi == 0)
    def _():
        parts = jnp.concatenate(_split3_bf16(-ccol_ref[0]), axis=1).astype(BF16)
        r = lax.broadcasted_iota(jnp.int32, (3 * LANES, LANES), 0)
        c = lax.broadcasted_iota(jnp.int32, (3 * LANES, LANES), 1)
        head_i = r % LANES - hp * n_heads
        pick = (head_i >= 0) & (head_i < n_heads) & (c == 3 * head_i + r // LANES)
        sel = jnp.where(pick, 1.0, 0.0).astype(BF16)
        spec = jnp.dot(parts, sel, preferred_element_type=F32)
        spec = jnp.where((lane >= ones0) & (lane < ones0 + 3), 1.0, spec)
        spec_ref[...] = spec.astype(BF16)

    ct_all = ccol_ref[0, pl.ds(pl.multiple_of(qi * tq, tq), tq), :]
    lane_c = lax.broadcasted_iota(jnp.int32, ct_all.shape, 1)
    q_heads = []
    for i, (pr, e) in enumerate(heads):
        qp = q_ref[0, :, pr * LANES:(pr + 1) * LANES]
        own = lo_half if e == 0 else jnp.logical_not(lo_half)
        qa = jnp.where(own, qp, jnp.zeros_like(qp))
        ct = jnp.sum(jnp.where(lane_c == hp * n_heads + i, ct_all, 0.0), axis=1, keepdims=True)
        hi, mid, lo = _split3_bf16(ct)
        qs = jnp.where((lane >= 3 * i) & (lane < 3 * i + 3), 1.0, 0.0)
        qs = jnp.where(lane == ones0, hi, qs)
        qs = jnp.where(lane == ones0 + 1, mid, qs)
        qs = jnp.where(lane == ones0 + 2, lo, qs)
        q_heads.append(jnp.concatenate([qa, qs.astype(BF16)], axis=1))
    rr = lax.broadcasted_iota(jnp.int32, (tq, tq), 0)
    cc_i = lax.broadcasted_iota(jnp.int32, (tq, tq), 1)
    causal = rr >= cc_i
    nchunk = tq // LANES

    def chunks(a):
        return [a[:, i * LANES:(i + 1) * LANES] for i in range(nchunk)]

    def attend(n):
        outs = []
        for idx, (pr, e) in enumerate(heads):
            m = None
            l_part = jnp.zeros((tq, LANES), F32)
            o = jnp.zeros((tq, LANES), F32)
            for j in range(n + 1):
                rows = slice(j * tq, (j + 1) * tq)
                kj = jnp.concatenate([k_ref[0, rows, pr * LANES:(pr + 1) * LANES],
                                      spec_ref[rows, :]], axis=1)
                s = lax.dot_general(q_heads[idx], kj, (((1,), (1,)), ((), ())),
                                    preferred_element_type=F32)
                if j == n:
                    s = jnp.where(causal, s, NEG)
                bm = jnp.max(functools.reduce(jnp.maximum, chunks(s)), axis=1, keepdims=True)
                if m is None:
                    m = bm
                else:
                    m_new = jnp.maximum(m, bm)
                    a = jnp.exp2(m - m_new)
                    l_part = l_part * a
                    o = o * a
                    m = m_new
                p = jnp.exp2(s - m)
                l_part = l_part + functools.reduce(lambda a, b: a + b, chunks(p))
                o = o + jnp.dot(p.astype(BF16), v_ref[0, rows, pr * LANES:(pr + 1) * LANES],
                                preferred_element_type=F32)
            outs.append(o / jnp.sum(l_part, axis=1, keepdims=True))
        for pr in range(pps):
            o_ref[0, :, pr * LANES:(pr + 1) * LANES] = jnp.where(
                lo_half, outs[2 * pr], outs[2 * pr + 1]).astype(o_ref.dtype)

    lax.switch(qi, [functools.partial(attend, n) for n in range(nq)])


FOX_PAIRS_PER_STEP = 4


def _fox_attention(proj, ccol, batch, seq):
    tq = min(256, seq)
    nq = seq // tq
    hd_all = FOX_HEADS * HEAD_DIM
    pps = FOX_PAIRS_PER_STEP
    width = pps * LANES
    groups = hd_all // width
    kern = functools.partial(_fox_kernel, tq=tq, nq=nq, pps=pps)
    return pl.pallas_call(
        kern,
        out_shape=jax.ShapeDtypeStruct((batch, seq, hd_all), BF16),
        grid=(batch, groups, nq),
        in_specs=[pl.BlockSpec((1, tq, width), lambda b, h, i: (b, i, h)),
                  pl.BlockSpec((1, seq, width), lambda b, h, i: (b, 0, groups + h)),
                  pl.BlockSpec((1, seq, width), lambda b, h, i: (b, 0, 2 * groups + h)),
                  pl.BlockSpec((1, seq, LANES), lambda b, h, i: (b, 0, 0))],
        out_specs=pl.BlockSpec((1, tq, width), lambda b, h, i: (b, i, h)),
        scratch_shapes=[pltpu.VMEM((seq, LANES), BF16)],
        compiler_params=_cparams("parallel", "parallel", "arbitrary"),
        name="fox_attention",
    )(proj, proj, proj, ccol)


def _layer_norm(y, gain, bias):
    mu = jnp.mean(y, axis=-1, keepdims=True)
    d = y - mu
    var = jnp.mean(d * d, axis=-1, keepdims=True)
    return d * lax.rsqrt(var + LN_EPS) * gain + bias


def _split_bf16(v):
    hi = v.astype(BF16)
    lo = (v - hi.astype(F32)).astype(BF16)
    return hi, lo


def _route(logits_t):
    rows = [logits_t[e:e + 1, :] for e in range(N_EXPERTS)]
    mx = functools.reduce(jnp.maximum, rows)
    ex = [jnp.exp(v - mx) for v in rows]
    den = functools.reduce(lambda a, b: a + b, ex)
    sc = [v / den for v in ex]
    gscore = []
    for g in range(N_GROUPS):
        mem = sc[g * EXPERTS_PER_GROUP:(g + 1) * EXPERTS_PER_GROUP]
        pairs = [mem[i] + mem[j] for i in range(4) for j in range(i + 1, 4)]
        gscore.append(functools.reduce(jnp.maximum, pairs))
    gmax = functools.reduce(jnp.maximum, gscore)
    taken = None
    gates = []
    for g in range(N_GROUPS):
        eq = gscore[g] == gmax
        best = eq if taken is None else eq & jnp.logical_not(taken)
        taken = eq if taken is None else taken | eq
        mem = sc[g * EXPERTS_PER_GROUP:(g + 1) * EXPERTS_PER_GROUP]
        sel = []
        for i in range(4):
            rank = jnp.zeros_like(mem[i])
            for j in range(4):
                if j == i:
                    continue
                ahead = (mem[j] > mem[i]) | ((mem[j] == mem[i]) & (j < i))
                rank = rank + jnp.where(ahead, 1.0, 0.0)
            sel.append(best & (rank < 2.0))
        tot = functools.reduce(lambda a, b: a + b,
                               [jnp.where(sel[i], mem[i], 0.0) for i in range(4)])
        for i in range(4):
            gates.append(jnp.where(sel[i], mem[i] / tot, 0.0))
    return gates


OUTPROJ_SUBTILES = 2


def _outproj_kernel(a_ref, wo_ref, x_ref, g_ref, b_ref, wr_ref, br_ref, x1_ref, gate_ref):
    sub = x_ref.shape[0] // OUTPROJ_SUBTILES
    wo = wo_ref[...].astype(BF16)
    for h in range(OUTPROJ_SUBTILES):
        rows = slice(h * sub, (h + 1) * sub)
        mix = jnp.dot(a_ref[rows, :], wo, preferred_element_type=F32)
        x1 = _layer_norm(ALPHA * x_ref[rows, :] + mix, g_ref[...], b_ref[...])
        x1_ref[rows, :] = x1
        hi, lo = _split_bf16(x1)
        from_hi = jnp.dot(hi, wr_ref[...], preferred_element_type=F32)
        from_lo = jnp.dot(lo, wr_ref[...], preferred_element_type=F32)
        logits = (from_hi[:, :LANES] + (from_lo[:, :LANES] + from_hi[:, LANES:])) + br_ref[...]
        routed = _route(logits.T[:N_EXPERTS, :])
        pad = jnp.zeros((LANES - len(routed), sub), F32)
        gate_ref[rows, :] = jnp.concatenate(list(routed) + [pad], axis=0).T


def _outproj_ln_router(a2, wo_stack, layer, x2, gain, bias, wr, br):
    t_tokens, d = x2.shape
    tm = min(512, t_tokens)
    row = lambda i: (i, 0)
    fixed = lambda i: (0, 0)
    return pl.pallas_call(
        _outproj_kernel,
        out_shape=(jax.ShapeDtypeStruct((t_tokens, d), F32),
                   jax.ShapeDtypeStruct((t_tokens, LANES), F32)),
        grid=(t_tokens // tm,),
        in_specs=[pl.BlockSpec((tm, d), row),
                  pl.BlockSpec((None, d, d), lambda i: (layer, 0, 0)),
                  pl.BlockSpec((tm, d), row),
                  pl.BlockSpec((1, d), fixed),
                  pl.BlockSpec((1, d), fixed),
                  pl.BlockSpec((d, 2 * LANES), fixed),
                  pl.BlockSpec((1, LANES), fixed)],
        out_specs=(pl.BlockSpec((tm, d), row), pl.BlockSpec((tm, LANES), row)),
        compiler_params=_cparams("parallel"),
        name="outproj_ln_router",
    )(a2, wo_stack, x2, gain, bias, wr, br)


def _moe_kernel(x_ref, gate_ref, wg_ref, wu_ref, wd_ref, g_ref, b_ref, o_ref, xb_ref, acc_ref):
    grp = pl.program_id(1)

    @pl.when(grp == 0)
    def _():
        xb_ref[...] = x_ref[...].astype(BF16)
        acc_ref[...] = jnp.zeros_like(acc_ref)

    gates = gate_ref[...]
    lane = lax.broadcasted_iota(jnp.int32, gates.shape, 1)
    for k in range(EXPERTS_PER_GROUP):
        xb = xb_ref[...]
        hg = jnp.dot(xb, wg_ref[k].astype(BF16), preferred_element_type=F32)
        hu = jnp.dot(xb, wu_ref[k].astype(BF16), preferred_element_type=F32)
        ge = jnp.sum(jnp.where(lane == grp * EXPERTS_PER_GROUP + k, gates, 0.0),
                     axis=1, keepdims=True)
        act = hg * jax.nn.sigmoid(hg) * hu * ge
        acc_ref[...] += jnp.dot(act.astype(BF16), wd_ref[k].astype(BF16),
                                preferred_element_type=F32)

    @pl.when(grp == N_GROUPS - 1)
    def _():
        o_ref[...] = _layer_norm(ALPHA * x_ref[...] + acc_ref[...], g_ref[...], b_ref[...])


def _moe_ln(x2, gates, w_gate, w_up, w_down, layer, gain, bias):
    t_tokens, d = x2.shape
    tm = min(1024, t_tokens)
    row = lambda i, e: (i, 0)
    fixed = lambda i, e: (0, 0)
    per_group = lambda i, e: (layer, e, 0, 0)
    return pl.pallas_call(
        _moe_kernel,
        out_shape=jax.ShapeDtypeStruct((t_tokens, d), F32),
        grid=(t_tokens // tm, N_GROUPS),
        in_specs=[pl.BlockSpec((tm, d), row),
                  pl.BlockSpec((tm, LANES), row),
                  pl.BlockSpec((None, EXPERTS_PER_GROUP, d, D_EXPERT), per_group),
                  pl.BlockSpec((None, EXPERTS_PER_GROUP, d, D_EXPERT), per_group),
                  pl.BlockSpec((None, EXPERTS_PER_GROUP, D_EXPERT, d), per_group),
                  pl.BlockSpec((1, d), fixed),
                  pl.BlockSpec((1, d), fixed)],
        out_specs=pl.BlockSpec((tm, d), row),
        scratch_shapes=[pltpu.VMEM((tm, d), BF16), pltpu.VMEM((tm, d), F32)],
        compiler_params=_cparams("parallel", "arbitrary"),
        name="moe_ln",
    )(x2, gates, w_gate, w_up, w_down, gain, bias)


def _pad_lanes(w):
    return jnp.pad(w, ((0, 0), (0, LANES - w.shape[1])))


def kernel(x, ln_gain, ln_bias, swa_w_in, swa_sinks, swa_w_o, fox_w_in, fox_b_f, fox_w_o,
           w_router, b_router, w_gate, w_up, w_down):
    batch, seq, d = x.shape
    t_tokens = batch * seq
    tables = _rope_tables(seq)
    wr = jnp.concatenate(_split_bf16(_pad_lanes(w_router)), axis=1)
    br = _pad_lanes(b_router[None, :])
    fox_qkv = 3 * FOX_HEADS * HEAD_DIM
    x2 = x.reshape(t_tokens, d)
    for i in range(DEPTH):
        j = i // 2
        if i % 2 == 0:
            proj = _in_projection(x2, swa_w_in, j, swa_w_in.shape[2], seq,
                                  q_cols=SWA_Q_HEADS * HEAD_DIM, q_scale=QK_SCALE * LOG2E,
                                  rope_cols=(SWA_Q_HEADS + SWA_KV_HEADS) * HEAD_DIM,
                                  tables=tables)
            attn = _swa_attention(proj.reshape(batch, seq, -1), swa_sinks[j] * LOG2E, batch, seq)
            w_o = swa_w_o
        else:
            proj = _in_projection(x2, fox_w_in, j, fox_qkv, seq,
                                  q_cols=FOX_HEADS * HEAD_DIM, q_scale=QK_SCALE * LOG2E,
                                  rope_cols=0, tables=tables)
            ccol = _decay_cumsum(x2.reshape(batch, seq, d),
                                 _pad_lanes(fox_w_in[j, :, fox_qkv:]).astype(BF16),
                                 _pad_lanes(fox_b_f[j][None, :]))
            attn = _fox_attention(proj.reshape(batch, seq, -1), ccol, batch, seq)
            w_o = fox_w_o
        x1, gates = _outproj_ln_router(attn.reshape(t_tokens, d), w_o, j, x2,
                                       ln_gain[i, 0][None, :], ln_bias[i, 0][None, :],
                                       wr, br)
        x2 = _moe_ln(x1, gates, w_gate, w_up, w_down, i,
                     ln_gain[i, 1][None, :], ln_bias[i, 1][None, :])
    return x2.reshape(batch, seq, d)
```

```python
import functools
import math

import jax
import jax.numpy as jnp
from jax import lax
from jax.experimental import pallas as pl
from jax.experimental.pallas import tpu as pltpu

F32 = jnp.float32
BF16 = jnp.bfloat16

HEAD_DIM = 64
ROT_DIM = 16
ROT_HALF = ROT_DIM // 2
ROPE_THETA = 500000.0
SWA_Q_HEADS = 16
SWA_KV_HEADS = 4
SWA_BLOCK = 128
FOX_HEADS = 16
N_EXPERTS = 16
N_GROUPS = 4
EXPERTS_PER_GROUP = 4
D_EXPERT = 256
DEPTH = 4
ALPHA = (2.0 * DEPTH) ** 0.25
LN_EPS = 1e-5
NEG = -1e30
QK_SCALE = HEAD_DIM ** -0.5
LOG2E = math.log2(math.e)

LANES = 128
VMEM_LIMIT = 56 * 1024 * 1024


def _cparams(*sem):
    return pltpu.CompilerParams(dimension_semantics=sem, vmem_limit_bytes=VMEM_LIMIT)


def _inproj_kernel(x_ref, w_ref, cos_ref, sa_ref, sb_ref, o_ref, *, tn, q_cols, q_scale,
                   rope_cols):
    xb = x_ref[...].astype(BF16)
    n = o_ref.shape[1]
    for c0 in range(0, n, tn):
        acc = jnp.dot(xb, w_ref[:, c0:c0 + tn].astype(BF16), preferred_element_type=F32)
        n_rope = min(max(rope_cols - c0, 0), tn)
        if n_rope:
            reps = n_rope // LANES
            head = acc[:, :n_rope]
            head = (head * jnp.tile(cos_ref[...], (1, reps))
                    + pltpu.roll(head, ROT_HALF, 1) * jnp.tile(sa_ref[...], (1, reps))
                    + pltpu.roll(head, n_rope - ROT_HALF, 1) * jnp.tile(sb_ref[...], (1, reps)))
            acc = head if n_rope == tn else jnp.concatenate([head, acc[:, n_rope:]], axis=1)
        if c0 + tn <= q_cols:
            acc = acc * q_scale
        else:
            assert c0 >= q_cols, "q columns must end on a chunk boundary"
        o_ref[:, c0:c0 + tn] = acc.astype(o_ref.dtype)


def _rope_tables(seq):
    inv_freq = jnp.power(ROPE_THETA, -jnp.arange(ROT_HALF, dtype=F32) * (2.0 / ROT_DIM))
    ang = jnp.arange(seq, dtype=F32)[:, None] * inv_freq[None, :]
    cos, sin = jnp.cos(ang), jnp.sin(ang)
    ones = jnp.ones((seq, HEAD_DIM - ROT_DIM), F32)
    zeros = jnp.zeros((seq, HEAD_DIM - ROT_DIM), F32)
    z8 = jnp.zeros((seq, ROT_HALF), F32)
    cos_h = jnp.concatenate([cos, cos, ones], axis=1)
    sa_h = jnp.concatenate([z8, sin, zeros], axis=1)
    sb_h = jnp.concatenate([-sin, z8, zeros], axis=1)
    rep = LANES // HEAD_DIM
    return tuple(jnp.tile(t, (1, rep)) for t in (cos_h, sa_h, sb_h))


def _in_projection(x2, w_stack, layer, n, seq, *, q_cols, q_scale, rope_cols, tables):
    t_tokens, d = x2.shape
    tm = min(1024, seq)
    tn = 512
    assert t_tokens % tm == 0 and seq % tm == 0 and n % tn == 0
    pos_blocks = seq // tm
    kern = functools.partial(_inproj_kernel, tn=tn, q_cols=q_cols, q_scale=q_scale,
                             rope_cols=rope_cols)
    tab_spec = pl.BlockSpec((tm, LANES), lambda i: (i % pos_blocks, 0))
    return pl.pallas_call(
        kern,
        out_shape=jax.ShapeDtypeStruct((t_tokens, n), BF16),
        grid=(t_tokens // tm,),
        in_specs=[pl.BlockSpec((tm, d), lambda i: (i, 0)),
                  pl.BlockSpec((None,) + w_stack.shape[1:], lambda i: (layer, 0, 0),
                               pipeline_mode=pl.Buffered(1)),
                  tab_spec, tab_spec, tab_spec],
        out_specs=pl.BlockSpec((tm, n), lambda i: (i, 0)),
        compiler_params=_cparams("parallel"),
        name="in_projection",
    )(x2, w_stack, *tables)


def _swa_kernel(sink_ref, q_ref, kp_ref, kc_ref, vp_ref, vc_ref, o_ref):
    n = pl.program_id(1)
    blk = SWA_BLOCK
    lane = lax.broadcasted_iota(jnp.int32, (1, LANES), 1)
    lo = lane < HEAD_DIM
    r = lax.broadcasted_iota(jnp.int32, (blk, 2 * blk), 0)
    c = lax.broadcasted_iota(jnp.int32, (blk, 2 * blk), 1)
    allowed = (c > r) & (c <= r + blk) & ((c >= blk) | (n > 0))
    allowed2 = jnp.concatenate([allowed, allowed], axis=0)
    top_rows = lax.broadcasted_iota(jnp.int32, (2 * blk, 1), 0) < blk
    k_all = jnp.concatenate([kp_ref[0], kc_ref[0]], axis=0).astype(F32)
    v_all = jnp.concatenate([vp_ref[0], vc_ref[0]], axis=0).astype(F32)
    for g in range(SWA_KV_HEADS):
        grp = g // 2
        kf = k_all[:, grp * LANES:(grp + 1) * LANES]
        vf = v_all[:, grp * LANES:(grp + 1) * LANES]
        kr = pltpu.roll(kf, HEAD_DIM, 1)
        vr = pltpu.roll(vf, HEAD_DIM, 1)
        own_lo = (g % 2 == 0)
        k_src_lo, k_src_hi = (kf, kr) if own_lo else (kr, kf)
        k_lo = jnp.where(lo, k_src_lo, 0.0).astype(BF16)
        k_hi = jnp.where(lo, 0.0, k_src_hi).astype(BF16)
        v_lo = jnp.where(lo, vf if own_lo else vr, 1.0).astype(BF16)
        v_hi = jnp.where(lo, 1.0, vr if own_lo else vf).astype(BF16)
        p0 = 2 * g
        qp = jnp.concatenate([q_ref[0, :, p0 * LANES:(p0 + 1) * LANES],
                              q_ref[0, :, (p0 + 1) * LANES:(p0 + 2) * LANES]], axis=0)
        outs, sink_terms = [], []
        for e, (kk, vv) in enumerate(((k_lo, v_lo), (k_hi, v_hi))):
            s = lax.dot_general(qp, kk, (((1,), (1,)), ((), ())), preferred_element_type=F32)
            s = jnp.where(allowed2, s, NEG)
            sink = jnp.where(top_rows, sink_ref[2 * p0 + e], sink_ref[2 * p0 + 2 + e])
            m = jnp.maximum(jnp.max(s, axis=1, keepdims=True), sink)
            pr = jnp.exp2(s - m)
            outs.append(jnp.dot(pr.astype(BF16), vv, preferred_element_type=F32))
            sink_terms.append(jnp.exp2(sink - m))
        num = jnp.where(lo, outs[0], outs[1])
        den = (pltpu.roll(jnp.where(lo, outs[1], outs[0]), HEAD_DIM, 1)
               + jnp.where(lo, sink_terms[0], sink_terms[1]))
        out = (num / den).astype(o_ref.dtype)
        o_ref[0, :, p0 * LANES:(p0 + 1) * LANES] = out[:blk]
        o_ref[0, :, (p0 + 1) * LANES:(p0 + 2) * LANES] = out[blk:]


def _swa_attention(proj, sinks, batch, seq):
    blk = SWA_BLOCK
    nb = seq // blk
    qd = SWA_Q_HEADS * HEAD_DIM
    kd = SWA_KV_HEADS * HEAD_DIM
    kcol, vcol = qd // kd, qd // kd + 1
    prev = lambda b, n: jnp.maximum(n - 1, 0)
    return pl.pallas_call(
        _swa_kernel,
        out_shape=jax.ShapeDtypeStruct((batch, seq, qd), BF16),
        grid=(batch, nb),
        in_specs=[pl.BlockSpec(memory_space=pltpu.SMEM),
                  pl.BlockSpec((1, blk, qd), lambda b, n: (b, n, 0)),
                  pl.BlockSpec((1, blk, kd), lambda b, n: (b, prev(b, n), kcol)),
                  pl.BlockSpec((1, blk, kd), lambda b, n: (b, n, kcol)),
                  pl.BlockSpec((1, blk, kd), lambda b, n: (b, prev(b, n), vcol)),
                  pl.BlockSpec((1, blk, kd), lambda b, n: (b, n, vcol))],
        out_specs=pl.BlockSpec((1, blk, qd), lambda b, n: (b, n, 0)),
        compiler_params=_cparams("parallel", "arbitrary"),
        name="swa_attention",
    )(sinks, proj, proj, proj, proj, proj)


def _decay_kernel(x_ref, wf_ref, bf_ref, ccol_ref):
    seq = x_ref.shape[1]
    f = jnp.dot(x_ref[0].astype(BF16), wf_ref[...], preferred_element_type=F32) + bf_ref[...]
    ls = jnp.minimum(f, 0.0) - jnp.log1p(jnp.exp(-jnp.abs(f)))
    row = lax.broadcasted_iota(jnp.int32, ls.shape, 0)
    c = ls
    k = 1
    while k < seq:
        c = c + jnp.where(row >= k, pltpu.roll(c, k, 0), 0.0)
        k *= 2
    ccol_ref[0] = c * LOG2E


def _decay_cumsum(x3, wf, bfv):
    batch, seq, d = x3.shape
    return pl.pallas_call(
        _decay_kernel,
        out_shape=jax.ShapeDtypeStruct((batch, seq, LANES), F32),
        grid=(batch,),
        in_specs=[pl.BlockSpec((1, seq, d), lambda b: (b, 0, 0)),
                  pl.BlockSpec((d, LANES), lambda b: (0, 0)),
                  pl.BlockSpec((1, LANES), lambda b: (0, 0))],
        out_specs=pl.BlockSpec((1, seq, LANES), lambda b: (b, 0, 0)),
        compiler_params=_cparams("parallel"),
        name="fox_decay_cumsum",
    )(x3, wf, bfv)


def _split3_bf16(v):
    hi = v.astype(BF16).astype(F32)
    r = v - hi
    mid = r.astype(BF16).astype(F32)
    lo = (r - mid).astype(BF16).astype(F32)
    return hi, mid, lo


def _fox_kernel(q_ref, k_ref, v_ref, ccol_ref, o_ref, spec_ref, *, tq, nq, pps):
    hp = pl.program_id(1)
    qi = pl.program_id(2)
    n_heads = 2 * pps
    ones0 = 3 * n_heads
    lane = lax.broadcasted_iota(jnp.int32, (1, LANES), 1)
    lo_half = lane < HEAD_DIM
    heads = [(pr, e) for pr in range(pps) for e in range(2)]

    @pl.when(qi == 0)
    def _():
        parts = jnp.concatenate(_split3_bf16(-ccol_ref[0]), axis=1).astype(BF16)
        r = lax.broadcasted_iota(jnp.int32, (3 * LANES, LANES), 0)
        c = lax.broadcasted_iota(jnp.int32, (3 * LANES, LANES), 1)
        head_i = r % LANES - hp * n_heads
        pick = (head_i >= 0) & (head_i < n_heads) & (c == 3 * head_i + r // LANES)
        sel = jnp.where(pick, 1.0, 0.0).astype(BF16)
        spec = jnp.dot(parts, sel, preferred_element_type=F32)
        spec = jnp.where((lane >= ones0) & (lane < ones0 + 3), 1.0, spec)
        spec_ref[...] = spec.astype(BF16)

    ct_all = ccol_ref[0, pl.ds(pl.multiple_of(qi * tq, tq), tq), :]
    lane_c = lax.broadcasted_iota(jnp.int32, ct_all.shape, 1)
    q_heads = []
    for i, (pr, e) in enumerate(heads):
        qp = q_ref[0, :, pr * LANES:(pr + 1) * LANES]
        own = lo_half if e == 0 else jnp.logical_not(lo_half)
        qa = jnp.where(own, qp, jnp.zeros_like(qp))
        ct = jnp.sum(jnp.where(lane_c == hp * n_heads + i, ct_all, 0.0), axis=1, keepdims=True)
        hi, mid, lo = _split3_bf16(ct)
        qs = jnp.where((lane >= 3 * i) & (lane < 3 * i + 3), 1.0, 0.0)
        qs = jnp.where(lane == ones0, hi, qs)
        qs = jnp.where(lane == ones0 + 1, mid, qs)
        qs = jnp.where(lane == ones0 + 2, lo, qs)
        q_heads.append(jnp.concatenate([qa, qs.astype(BF16)], axis=1))
    rr = lax.broadcasted_iota(jnp.int32, (tq, tq), 0)
    cc_i = lax.broadcasted_iota(jnp.int32, (tq, tq), 1)
    causal = rr >= cc_i
    nchunk = tq // LANES

    def chunks(a):
        return [a[:, i * LANES:(i + 1) * LANES] for i in range(nchunk)]

    def attend(n):
        outs = []
        for idx, (pr, e) in enumerate(heads):
            m = None
            l_part = jnp.zeros((tq, LANES), F32)
            o = jnp.zeros((tq, LANES), F32)
            for j in range(n + 1):
                rows = slice(j * tq, (j + 1) * tq)
                kj = jnp.concatenate([k_ref[0, rows, pr * LANES:(pr + 1) * LANES],
                                      spec_ref[rows, :]], axis=1)
                s = lax.dot_general(q_heads[idx], kj, (((1,), (1,)), ((), ())),
                                    preferred_element_type=F32)
                if j == n:
                    s = jnp.where(causal, s, NEG)
                bm = jnp.max(functools.reduce(jnp.maximum, chunks(s)), axis=1, keepdims=True)
                if m is None:
                    m = bm
                else:
                    m_new = jnp.maximum(m, bm)
                    a = jnp.exp2(m - m_new)
                    l_part = l_part * a
                    o = o * a
                    m = m_new
                p = jnp.exp2(s - m)
                l_part = l_part + functools.reduce(lambda a, b: a + b, chunks(p))
                o = o + jnp.dot(p.astype(BF16), v_ref[0, rows, pr * LANES:(pr + 1) * LANES],
                                preferred_element_type=F32)
            outs.append(o / jnp.sum(l_part, axis=1, keepdims=True))
        for pr in range(pps):
            o_ref[0, :, pr * LANES:(pr + 1) * LANES] = jnp.where(
                lo_half, outs[2 * pr], outs[2 * pr + 1]).astype(o_ref.dtype)

    lax.switch(qi, [functools.partial(attend, n) for n in range(nq)])


FOX_PAIRS_PER_STEP = 4


def _fox_attention(proj, ccol, batch, seq):
    tq = min(256, seq)
    nq = seq // tq
    hd_all = FOX_HEADS * HEAD_DIM
    pps = FOX_PAIRS_PER_STEP
    width = pps * LANES
    groups = hd_all // width
    kern = functools.partial(_fox_kernel, tq=tq, nq=nq, pps=pps)
    return pl.pallas_call(
        kern,
        out_shape=jax.ShapeDtypeStruct((batch, seq, hd_all), BF16),
        grid=(batch, groups, nq),
        in_specs=[pl.BlockSpec((1, tq, width), lambda b, h, i: (b, i, h)),
                  pl.BlockSpec((1, seq, width), lambda b, h, i: (b, 0, groups + h)),
                  pl.BlockSpec((1, seq, width), lambda b, h, i: (b, 0, 2 * groups + h)),
                  pl.BlockSpec((1, seq, LANES), lambda b, h, i: (b, 0, 0))],
        out_specs=pl.BlockSpec((1, tq, width), lambda b, h, i: (b, i, h)),
        scratch_shapes=[pltpu.VMEM((seq, LANES), BF16)],
        compiler_params=_cparams("parallel", "parallel", "arbitrary"),
        name="fox_attention",
    )(proj, proj, proj, ccol)


def _layer_norm(y, gain, bias):
    mu = jnp.mean(y, axis=-1, keepdims=True)
    d = y - mu
    var = jnp.mean(d * d, axis=-1, keepdims=True)
    return d * lax.rsqrt(var + LN_EPS) * gain + bias


def _split_bf16(v):
    hi = v.astype(BF16)
    lo = (v - hi.astype(F32)).astype(BF16)
    return hi, lo


EXPERT_PAIRS = [(i, j) for i in range(EXPERTS_PER_GROUP) for j in range(i + 1, EXPERTS_PER_GROUP)]
N_BUCKETS = N_GROUPS * len(EXPERT_PAIRS)


def _route(logits_t):
    rows = [logits_t[e:e + 1, :] for e in range(N_EXPERTS)]
    mx = functools.reduce(jnp.maximum, rows)
    ex = [jnp.exp(v - mx) for v in rows]
    den = functools.reduce(lambda a, b: a + b, ex)
    sc = [v / den for v in ex]
    gscore = []
    for g in range(N_GROUPS):
        mem = sc[g * EXPERTS_PER_GROUP:(g + 1) * EXPERTS_PER_GROUP]
        pairs = [mem[i] + mem[j] for i in range(4) for j in range(i + 1, 4)]
        gscore.append(functools.reduce(jnp.maximum, pairs))
    gmax = functools.reduce(jnp.maximum, gscore)
    taken = None
    bucket = jnp.zeros_like(gmax)
    gate_a = jnp.zeros_like(gmax)
    gate_b = jnp.zeros_like(gmax)
    for g in range(N_GROUPS):
        eq = gscore[g] == gmax
        best = eq if taken is None else eq & jnp.logical_not(taken)
        taken = eq if taken is None else taken | eq
        mem = sc[g * EXPERTS_PER_GROUP:(g + 1) * EXPERTS_PER_GROUP]
        sel = []
        for i in range(4):
            rank = jnp.zeros_like(mem[i])
            for j in range(4):
                if j == i:
                    continue
                ahead = (mem[j] > mem[i]) | ((mem[j] == mem[i]) & (j < i))
                rank = rank + jnp.where(ahead, 1.0, 0.0)
            sel.append(best & (rank < 2.0))
        tot = functools.reduce(lambda a, b: a + b,
                               [jnp.where(sel[i], mem[i], 0.0) for i in range(4)])
        seen = jnp.zeros_like(gmax)
        for i in range(4):
            gate = mem[i] / tot
            gate_a = gate_a + jnp.where(sel[i] & (seen == 0.0), gate, 0.0)
            gate_b = gate_b + jnp.where(sel[i] & (seen == 1.0), gate, 0.0)
            seen = seen + jnp.where(sel[i], 1.0, 0.0)
        for p, (i, j) in enumerate(EXPERT_PAIRS):
            bucket = bucket + jnp.where(sel[i] & sel[j], float(g * len(EXPERT_PAIRS) + p), 0.0)
    return bucket, gate_a, gate_b


OUTPROJ_SUBTILES = 2


ROW_W = 1024 + LANES


def _outproj_kernel(a_ref, wo_ref, x_ref, g_ref, b_ref, wr_ref, br_ref, x1p_ref):
    sub = x_ref.shape[0] // OUTPROJ_SUBTILES
    d = x_ref.shape[1]
    wo = wo_ref[...].astype(BF16)
    for h in range(OUTPROJ_SUBTILES):
        rows = slice(h * sub, (h + 1) * sub)
        mix = jnp.dot(a_ref[rows, :], wo, preferred_element_type=F32)
        x1 = _layer_norm(ALPHA * x_ref[rows, :] + mix, g_ref[...], b_ref[...])
        x1p_ref[rows, :d] = x1
        hi, lo = _split_bf16(x1)
        from_hi = jnp.dot(hi, wr_ref[...], preferred_element_type=F32)
        from_lo = jnp.dot(lo, wr_ref[...], preferred_element_type=F32)
        logits = (from_hi[:, :LANES] + (from_lo[:, :LANES] + from_hi[:, LANES:])) + br_ref[...]
        routed = _route(logits.T[:N_EXPERTS, :])
        pad = jnp.zeros((LANES - len(routed), sub), F32)
        x1p_ref[rows, d:] = jnp.concatenate(list(routed) + [pad], axis=0).T


def _outproj_ln_router(a2, wo_stack, layer, x2, gain, bias, wr, br):
    t_tokens, d = x2.shape
    tm = min(1024, t_tokens)
    row = lambda i: (i, 0)
    fixed = lambda i: (0, 0)
    return pl.pallas_call(
        _outproj_kernel,
        out_shape=jax.ShapeDtypeStruct((t_tokens, ROW_W), F32),
        grid=(t_tokens // tm,),
        in_specs=[pl.BlockSpec((tm, d), row),
                  pl.BlockSpec((None, d, d), lambda i: (layer, 0, 0)),
                  pl.BlockSpec((tm, d), row),
                  pl.BlockSpec((1, d), fixed),
                  pl.BlockSpec((1, d), fixed),
                  pl.BlockSpec((d, 2 * LANES), fixed),
                  pl.BlockSpec((1, LANES), fixed)],
        out_specs=pl.BlockSpec((tm, ROW_W), row),
        compiler_params=_cparams("parallel"),
        name="outproj_ln_router",
    )(a2, wo_stack, x2, gain, bias, wr, br)


MOE_TILE = 256
D_MODEL = ROW_W - LANES
ROUTE_BLOCK = D_MODEL // LANES


def _slot_kernel(route_ref, pos_ref, cnt_ref, cnt_scr, start_scr, run_scr):
    phase = pl.program_id(0)
    i = pl.program_id(1)
    route = route_ref[...]
    tile = route.shape[0]
    lane = lax.broadcasted_iota(jnp.int32, route.shape, 1)
    onehot = lane.astype(F32) == route[:, 0:1]
    oh = jnp.where(onehot, 1.0, 0.0)

    @pl.when((phase == 0) & (i == 0))
    def _():
        cnt_scr[...] = jnp.zeros_like(cnt_scr)

    @pl.when(phase == 0)
    def _():
        cnt_scr[...] += jnp.sum(oh, axis=0, keepdims=True)

    @pl.when((phase == 1) & (i == 0))
    def _():
        cnt = jnp.broadcast_to(cnt_scr[...], start_scr.shape)
        padded = jnp.floor((cnt + (MOE_TILE - 1)) * (1.0 / MOE_TILE)) * MOE_TILE
        lane8 = lax.broadcasted_iota(jnp.int32, start_scr.shape, 1)
        incl = padded
        k = 1
        while k < LANES:
            incl = incl + jnp.where(lane8 >= k, pltpu.roll(incl, k, 1), 0.0)
            k *= 2
        start_scr[...] = incl - padded
        run_scr[...] = jnp.zeros_like(run_scr)

    @pl.when(phase == 1)
    def _():
        rr = lax.broadcasted_iota(jnp.int32, (tile, tile), 0)
        cc = lax.broadcasted_iota(jnp.int32, (tile, tile), 1)
        earlier = jnp.where(rr > cc, 1.0, 0.0).astype(BF16)
        before = (jnp.dot(earlier, oh.astype(BF16), preferred_element_type=F32)
                  + run_scr[...] + start_scr[0:1, :])
        slot = jnp.sum(jnp.where(onehot, before, 0.0), axis=1, keepdims=True)
        pos_ref[...] = jnp.broadcast_to(slot, route.shape)
        run_scr[...] += jnp.sum(oh, axis=0, keepdims=True)

    cnt_ref[...] = cnt_scr[...]


def _token_slots(x1p):
    t_tokens = x1p.shape[0]
    tile = min(1024, t_tokens)
    return pl.pallas_call(
        _slot_kernel,
        out_shape=(jax.ShapeDtypeStruct((t_tokens, LANES), F32),
                   jax.ShapeDtypeStruct((1, LANES), F32)),
        grid=(2, t_tokens // tile),
        in_specs=[pl.BlockSpec((tile, LANES), lambda p, i: (i, ROUTE_BLOCK))],
        out_specs=(pl.BlockSpec((tile, LANES), lambda p, i: (i * p, 0)),
                   pl.BlockSpec((1, LANES), lambda p, i: (0, 0))),
        scratch_shapes=[pltpu.VMEM((1, LANES), F32), pltpu.VMEM((8, LANES), F32),
                        pltpu.VMEM((1, LANES), F32)],
        compiler_params=_cparams("arbitrary", "arbitrary"),
        name="moe_token_slots",
    )(x1p)


def _tile_plan(cnt, n_tiles):
    counts = cnt[0, :N_BUCKETS].astype(jnp.int32)
    padded = (counts + MOE_TILE - 1) // MOE_TILE * MOE_TILE
    ends = jnp.cumsum(padded)
    n_used = ends[-1] // MOE_TILE
    tile_start = jnp.arange(n_tiles, dtype=jnp.int32) * MOE_TILE
    tile_bucket = jnp.minimum(jnp.sum(ends[None, :] <= tile_start[:, None], axis=1), N_BUCKETS - 1)
    pair_a = jnp.array([p[0] for p in EXPERT_PAIRS], jnp.int32)
    pair_b = jnp.array([p[1] for p in EXPERT_PAIRS], jnp.int32)
    grp = tile_bucket // len(EXPERT_PAIRS)
    pair = tile_bucket % len(EXPERT_PAIRS)
    tile_ea = grp * EXPERTS_PER_GROUP + pair_a[pair]
    tile_eb = grp * EXPERTS_PER_GROUP + pair_b[pair]
    tile_src = jnp.minimum(jnp.arange(n_tiles, dtype=jnp.int32), n_used - 1)
    return ends, padded, tile_ea, tile_eb, tile_src, n_used.reshape(1)


def _scatter_kernel(pos_ref, ends_ref, padded_ref, x_ref, xs_ref, zero_ref, sem, zsem):
    i = pl.program_id(0)
    tm = x_ref.shape[0]

    @pl.when(i == 0)
    def _():
        zero_ref[...] = jnp.zeros_like(zero_ref)
        n_slots = xs_ref.shape[0]
        tail_start = n_slots - N_BUCKETS * MOE_TILE

        def fill(start):
            return pltpu.make_async_copy(zero_ref, xs_ref.at[pl.ds(start, MOE_TILE), :], zsem)

        def bucket_tail(b):
            return pl.multiple_of(ends_ref[b] - MOE_TILE, MOE_TILE)

        def bucket_needs_fill(b):
            return (padded_ref[b] > 0) & (ends_ref[b] <= tail_start)

        for start in range(tail_start, n_slots, MOE_TILE):
            fill(start).start()
        for b in range(N_BUCKETS):
            @pl.when(bucket_needs_fill(b))
            def _():
                fill(bucket_tail(b)).start()
        for start in range(tail_start, n_slots, MOE_TILE):
            fill(start).wait()
        for b in range(N_BUCKETS):
            @pl.when(bucket_needs_fill(b))
            def _():
                fill(bucket_tail(b)).wait()

    def row_copy(r):
        slot = pos_ref[i * tm + r]
        return pltpu.make_async_copy(x_ref.at[pl.ds(r, 1), :], xs_ref.at[pl.ds(slot, 1), :], sem)

    def start(r, carry):
        row_copy(r).start()
        return carry

    def wait(r, carry):
        row_copy(r).wait()
        return carry

    lax.fori_loop(0, tm, start, 0, unroll=8)
    lax.fori_loop(0, tm, wait, 0, unroll=8)


def _scatter_rows(pos, ends, padded, x1p, n_slots):
    t_tokens = x1p.shape[0]
    tm = min(512, t_tokens)
    grid_spec = pltpu.PrefetchScalarGridSpec(
        num_scalar_prefetch=3,
        grid=(t_tokens // tm,),
        in_specs=[pl.BlockSpec((tm, ROW_W), lambda i, *_: (i, 0))],
        out_specs=pl.BlockSpec(memory_space=pl.ANY),
        scratch_shapes=[pltpu.VMEM((MOE_TILE, ROW_W), F32),
                        pltpu.SemaphoreType.DMA, pltpu.SemaphoreType.DMA],
    )
    return pl.pallas_call(
        _scatter_kernel,
        out_shape=jax.ShapeDtypeStruct((n_slots, ROW_W), F32),
        grid_spec=grid_spec,
        compiler_params=_cparams("arbitrary"),
        name="moe_scatter_rows",
    )(pos, ends, padded, x1p)


def _expert_kernel(ea_ref, eb_ref, src_ref, nu_ref, xs_ref, wga_ref, wua_ref, wda_ref,
                   wgb_ref, wub_ref, wdb_ref, ys_ref):
    k = pl.program_id(0)
    d = ys_ref.shape[1]

    @pl.when(k < nu_ref[0])
    def _():
        xb = xs_ref[:, :d].astype(BF16)
        y = jnp.zeros(ys_ref.shape, F32)
        for lane, (wg, wu, wd) in ((1, (wga_ref, wua_ref, wda_ref)), (2, (wgb_ref, wub_ref, wdb_ref))):
            gate = xs_ref[:, d + lane:d + lane + 1]
            hg = jnp.dot(xb, wg[...].astype(BF16), preferred_element_type=F32)
            hu = jnp.dot(xb, wu[...].astype(BF16), preferred_element_type=F32)
            act = hg * jax.nn.sigmoid(hg) * hu * gate
            y = y + jnp.dot(act.astype(BF16), wd[...].astype(BF16), preferred_element_type=F32)
        ys_ref[...] = y

    @pl.when(k >= nu_ref[0])
    def _():
        ys_ref[...] = jnp.zeros_like(ys_ref)


def _run_experts(tile_ea, tile_eb, tile_src, n_used, xs, w_gate, w_up, w_down, layer):
    n_slots = xs.shape[0]
    d = w_gate.shape[2]
    n_tiles = n_slots // MOE_TILE
    up_a = pl.BlockSpec((None, None, d, D_EXPERT), lambda k, ea, eb, src, nu: (layer, ea[k], 0, 0))
    up_b = pl.BlockSpec((None, None, d, D_EXPERT), lambda k, ea, eb, src, nu: (layer, eb[k], 0, 0))
    down_a = pl.BlockSpec((None, None, D_EXPERT, d), lambda k, ea, eb, src, nu: (layer, ea[k], 0, 0))
    down_b = pl.BlockSpec((None, None, D_EXPERT, d), lambda k, ea, eb, src, nu: (layer, eb[k], 0, 0))
    grid_spec = pltpu.PrefetchScalarGridSpec(
        num_scalar_prefetch=4,
        grid=(n_tiles,),
        in_specs=[pl.BlockSpec((MOE_TILE, ROW_W), lambda k, ea, eb, src, nu: (src[k], 0)),
                  up_a, up_a, down_a, up_b, up_b, down_b],
        out_specs=pl.BlockSpec((MOE_TILE, d), lambda k, ea, eb, src, nu: (k, 0)),
    )
    return pl.pallas_call(
        _expert_kernel,
        out_shape=jax.ShapeDtypeStruct((n_slots, d), F32),
        grid_spec=grid_spec,
        compiler_params=_cparams("arbitrary"),
        name="moe_experts",
    )(tile_ea, tile_eb, tile_src, n_used, xs, w_gate, w_up, w_down, w_gate, w_up, w_down)


def _gather_ln_kernel(pos_ref, x_ref, ys_ref, g_ref, b_ref, o_ref, ybuf_ref, sems):
    i = pl.program_id(0)
    n_steps = pl.num_programs(0)
    tm = x_ref.shape[0]
    cur = i % 2

    def row_copy(step, buf, r):
        slot = pos_ref[step * tm + r]
        return pltpu.make_async_copy(ys_ref.at[pl.ds(slot, 1), :],
                                     ybuf_ref.at[buf, pl.ds(r, 1), :], sems.at[buf])

    def request(step, buf):
        def body(r, carry):
            row_copy(step, buf, r).start()
            return carry
        lax.fori_loop(0, tm, body, 0, unroll=8)

    @pl.when(i == 0)
    def _():
        request(0, 0)

    @pl.when(i + 1 < n_steps)
    def _():
        request(i + 1, 1 - cur)

    def arrive(r, carry):
        row_copy(i, cur, r).wait()
        return carry

    lax.fori_loop(0, tm, arrive, 0, unroll=8)
    o_ref[...] = _layer_norm(ALPHA * x_ref[...] + ybuf_ref[cur], g_ref[...], b_ref[...])


def _gather_ln(pos, x1p, ys, gain, bias):
    t_tokens = x1p.shape[0]
    d = ys.shape[1]
    tm = min(512, t_tokens)
    grid_spec = pltpu.PrefetchScalarGridSpec(
        num_scalar_prefetch=1,
        grid=(t_tokens // tm,),
        in_specs=[pl.BlockSpec((tm, d), lambda i, pos: (i, 0)),
                  pl.BlockSpec(memory_space=pl.ANY),
                  pl.BlockSpec((1, d), lambda i, pos: (0, 0)),
                  pl.BlockSpec((1, d), lambda i, pos: (0, 0))],
        out_specs=pl.BlockSpec((tm, d), lambda i, pos: (i, 0)),
        scratch_shapes=[pltpu.VMEM((2, tm, d), F32), pltpu.SemaphoreType.DMA((2,))],
    )
    return pl.pallas_call(
        _gather_ln_kernel,
        out_shape=jax.ShapeDtypeStruct((t_tokens, d), F32),
        grid_spec=grid_spec,
        compiler_params=_cparams("arbitrary"),
        name="moe_gather_ln",
    )(pos, x1p, ys, gain, bias)


def _moe_ln(x1p, w_gate, w_up, w_down, layer, gain, bias):
    t_tokens = x1p.shape[0]
    n_tiles = t_tokens // MOE_TILE + N_BUCKETS
    pos_f, cnt = _token_slots(x1p)
    pos = pos_f[:, 0].astype(jnp.int32)
    ends, padded, tile_ea, tile_eb, tile_src, n_used = _tile_plan(cnt, n_tiles)
    xs = _scatter_rows(pos, ends, padded, x1p, n_tiles * MOE_TILE)
    ys = _run_experts(tile_ea, tile_eb, tile_src, n_used, xs, w_gate, w_up, w_down, layer)
    return _gather_ln(pos, x1p, ys, gain, bias)


def _pad_lanes(w):
    return jnp.pad(w, ((0, 0), (0, LANES - w.shape[1])))


def kernel(x, ln_gain, ln_bias, swa_w_in, swa_sinks, swa_w_o, fox_w_in, fox_b_f, fox_w_o,
           w_router, b_router, w_gate, w_up, w_down):
    batch, seq, d = x.shape
    t_tokens = batch * seq
    tables = _rope_tables(seq)
    wr = jnp.concatenate(_split_bf16(_pad_lanes(w_router)), axis=1)
    br = _pad_lanes(b_router[None, :])
    fox_qkv = 3 * FOX_HEADS * HEAD_DIM
    x2 = x.reshape(t_tokens, d)
    for i in range(DEPTH):
        j = i // 2
        if i % 2 == 0:
            proj = _in_projection(x2, swa_w_in, j, swa_w_in.shape[2], seq,
                                  q_cols=SWA_Q_HEADS * HEAD_DIM, q_scale=QK_SCALE * LOG2E,
                                  rope_cols=(SWA_Q_HEADS + SWA_KV_HEADS) * HEAD_DIM,
                                  tables=tables)
            attn = _swa_attention(proj.reshape(batch, seq, -1), swa_sinks[j] * LOG2E, batch, seq)
            w_o = swa_w_o
        else:
            proj = _in_projection(x2, fox_w_in, j, fox_qkv, seq,
                                  q_cols=FOX_HEADS * HEAD_DIM, q_scale=QK_SCALE * LOG2E,
                                  rope_cols=0, tables=tables)
            ccol = _decay_cumsum(x2.reshape(batch, seq, d),
                                 _pad_lanes(fox_w_in[j, :, fox_qkv:]).astype(BF16),
                                 _pad_lanes(fox_b_f[j][None, :]))
            attn = _fox_attention(proj.reshape(batch, seq, -1), ccol, batch, seq)
            w_o = fox_w_o
        x1p = _outproj_ln_router(attn.reshape(t_tokens, d), w_o, j, x2,
                                 ln_gain[i, 0][None, :], ln_bias[i, 0][None, :], wr, br)
        x2 = _moe_ln(x1p, w_gate, w_up, w_down, i,
                     ln_gain[i, 1][None, :], ln_bias[i, 1][None, :])
    return x2.reshape(batch, seq, d)
```

```python
import functools
import math

import jax
import jax.numpy as jnp
from jax import lax
from jax.experimental import pallas as pl
from jax.experimental.pallas import tpu as pltpu

F32 = jnp.float32
BF16 = jnp.bfloat16

HEAD_DIM = 64
ROT_DIM = 16
ROT_HALF = ROT_DIM // 2
ROPE_THETA = 500000.0
SWA_Q_HEADS = 16
SWA_KV_HEADS = 4
SWA_BLOCK = 128
FOX_HEADS = 16
N_EXPERTS = 16
N_GROUPS = 4
EXPERTS_PER_GROUP = 4
D_EXPERT = 256
DEPTH = 4
ALPHA = (2.0 * DEPTH) ** 0.25
LN_EPS = 1e-5
NEG = -1e30
QK_SCALE = HEAD_DIM ** -0.5
LOG2E = math.log2(math.e)

LANES = 128
VMEM_LIMIT = 56 * 1024 * 1024


def _cparams(*sem):
    return pltpu.CompilerParams(dimension_semantics=sem, vmem_limit_bytes=VMEM_LIMIT)


def _inproj_kernel(x_ref, w_ref, cos_ref, sa_ref, sb_ref, o_ref, *, tn, q_cols, q_scale,
                   rope_cols):
    xb = x_ref[...].astype(BF16)
    n = o_ref.shape[1]
    for c0 in range(0, n, tn):
        acc = jnp.dot(xb, w_ref[:, c0:c0 + tn].astype(BF16), preferred_element_type=F32)
        n_rope = min(max(rope_cols - c0, 0), tn)
        if n_rope:
            reps = n_rope // LANES
            head = acc[:, :n_rope]
            head = (head * jnp.tile(cos_ref[...], (1, reps))
                    + pltpu.roll(head, ROT_HALF, 1) * jnp.tile(sa_ref[...], (1, reps))
                    + pltpu.roll(head, n_rope - ROT_HALF, 1) * jnp.tile(sb_ref[...], (1, reps)))
            acc = head if n_rope == tn else jnp.concatenate([head, acc[:, n_rope:]], axis=1)
        if c0 + tn <= q_cols:
            acc = acc * q_scale
        else:
            assert c0 >= q_cols, "q columns must end on a chunk boundary"
        o_ref[:, c0:c0 + tn] = acc.astype(o_ref.dtype)


def _rope_tables(seq):
    inv_freq = jnp.power(ROPE_THETA, -jnp.arange(ROT_HALF, dtype=F32) * (2.0 / ROT_DIM))
    ang = jnp.arange(seq, dtype=F32)[:, None] * inv_freq[None, :]
    cos, sin = jnp.cos(ang), jnp.sin(ang)
    ones = jnp.ones((seq, HEAD_DIM - ROT_DIM), F32)
    zeros = jnp.zeros((seq, HEAD_DIM - ROT_DIM), F32)
    z8 = jnp.zeros((seq, ROT_HALF), F32)
    cos_h = jnp.concatenate([cos, cos, ones], axis=1)
    sa_h = jnp.concatenate([z8, sin, zeros], axis=1)
    sb_h = jnp.concatenate([-sin, z8, zeros], axis=1)
    rep = LANES // HEAD_DIM
    return tuple(jnp.tile(t, (1, rep)) for t in (cos_h, sa_h, sb_h))


def _in_projection(x2, w_stack, layer, n, seq, *, q_cols, q_scale, rope_cols, tables):
    t_tokens, d = x2.shape
    tm = min(1024, seq)
    tn = 512
    assert t_tokens % tm == 0 and seq % tm == 0 and n % tn == 0
    pos_blocks = seq // tm
    kern = functools.partial(_inproj_kernel, tn=tn, q_cols=q_cols, q_scale=q_scale,
                             rope_cols=rope_cols)
    tab_spec = pl.BlockSpec((tm, LANES), lambda i: (i % pos_blocks, 0))
    return pl.pallas_call(
        kern,
        out_shape=jax.ShapeDtypeStruct((t_tokens, n), BF16),
        grid=(t_tokens // tm,),
        in_specs=[pl.BlockSpec((tm, d), lambda i: (i, 0)),
                  pl.BlockSpec((None,) + w_stack.shape[1:], lambda i: (layer, 0, 0),
                               pipeline_mode=pl.Buffered(1)),
                  tab_spec, tab_spec, tab_spec],
        out_specs=pl.BlockSpec((tm, n), lambda i: (i, 0)),
        compiler_params=_cparams("parallel"),
        name="in_projection",
    )(x2, w_stack, *tables)


def _swa_kernel(sink_ref, q_ref, kp_ref, kc_ref, vp_ref, vc_ref, o_ref):
    n = pl.program_id(1)
    blk = SWA_BLOCK
    lane = lax.broadcasted_iota(jnp.int32, (1, LANES), 1)
    lo = lane < HEAD_DIM
    r = lax.broadcasted_iota(jnp.int32, (blk, 2 * blk), 0)
    c = lax.broadcasted_iota(jnp.int32, (blk, 2 * blk), 1)
    allowed = (c > r) & (c <= r + blk) & ((c >= blk) | (n > 0))
    k_all = jnp.concatenate([kp_ref[0], kc_ref[0]], axis=0).astype(F32)
    v_all = jnp.concatenate([vp_ref[0], vc_ref[0]], axis=0).astype(F32)
    for g in range(SWA_KV_HEADS):
        grp = g // 2
        kf = k_all[:, grp * LANES:(grp + 1) * LANES]
        vf = v_all[:, grp * LANES:(grp + 1) * LANES]
        kr = pltpu.roll(kf, HEAD_DIM, 1)
        vr = pltpu.roll(vf, HEAD_DIM, 1)
        own_lo = (g % 2 == 0)
        k_src_lo, k_src_hi = (kf, kr) if own_lo else (kr, kf)
        k_lo = jnp.where(lo, k_src_lo, 0.0).astype(BF16)
        k_hi = jnp.where(lo, 0.0, k_src_hi).astype(BF16)
        v_lo = jnp.where(lo, vf if own_lo else vr, 1.0).astype(BF16)
        v_hi = jnp.where(lo, 1.0, vr if own_lo else vf).astype(BF16)
        for pp in range(2):
            p = 2 * g + pp
            qp = q_ref[0, :, p * LANES:(p + 1) * LANES]
            outs, sink_terms = [], []
            for e, (kk, vv) in enumerate(((k_lo, v_lo), (k_hi, v_hi))):
                s = lax.dot_general(qp, kk, (((1,), (1,)), ((), ())),
                                    preferred_element_type=F32)
                s = jnp.where(allowed, s, NEG)
                sink = sink_ref[2 * p + e]
                m = jnp.maximum(jnp.max(s, axis=1, keepdims=True), sink)
                pr = jnp.exp2(s - m)
                outs.append(jnp.dot(pr.astype(BF16), vv, preferred_element_type=F32))
                sink_terms.append(jnp.exp2(sink - m))
            num = jnp.where(lo, outs[0], outs[1])
            den = (pltpu.roll(jnp.where(lo, outs[1], outs[0]), HEAD_DIM, 1)
                   + jnp.where(lo, sink_terms[0], sink_terms[1]))
            o_ref[0, :, p * LANES:(p + 1) * LANES] = (num / den).astype(o_ref.dtype)


def _swa_attention(proj, sinks, batch, seq):
    blk = SWA_BLOCK
    nb = seq // blk
    qd = SWA_Q_HEADS * HEAD_DIM
    kd = SWA_KV_HEADS * HEAD_DIM
    kcol, vcol = qd // kd, qd // kd + 1
    prev = lambda b, n: jnp.maximum(n - 1, 0)
    return pl.pallas_call(
        _swa_kernel,
        out_shape=jax.ShapeDtypeStruct((batch, seq, qd), BF16),
        grid=(batch, nb),
        in_specs=[pl.BlockSpec(memory_space=pltpu.SMEM),
                  pl.BlockSpec((1, blk, qd), lambda b, n: (b, n, 0)),
                  pl.BlockSpec((1, blk, kd), lambda b, n: (b, prev(b, n), kcol)),
                  pl.BlockSpec((1, blk, kd), lambda b, n: (b, n, kcol)),
                  pl.BlockSpec((1, blk, kd), lambda b, n: (b, prev(b, n), vcol)),
                  pl.BlockSpec((1, blk, kd), lambda b, n: (b, n, vcol))],
        out_specs=pl.BlockSpec((1, blk, qd), lambda b, n: (b, n, 0)),
        compiler_params=_cparams("parallel", "arbitrary"),
        name="swa_attention",
    )(sinks, proj, proj, proj, proj, proj)


def _decay_kernel(x_ref, wf_ref, bf_ref, ccol_ref):
    seq = x_ref.shape[1]
    f = jnp.dot(x_ref[0].astype(BF16), wf_ref[...], preferred_element_type=F32) + bf_ref[...]
    ls = jnp.minimum(f, 0.0) - jnp.log1p(jnp.exp(-jnp.abs(f)))
    row = lax.broadcasted_iota(jnp.int32, ls.shape, 0)
    c = ls
    k = 1
    while k < seq:
        c = c + jnp.where(row >= k, pltpu.roll(c, k, 0), 0.0)
        k *= 2
    ccol_ref[0] = c * LOG2E


def _decay_cumsum(x3, wf, bfv):
    batch, seq, d = x3.shape
    return pl.pallas_call(
        _decay_kernel,
        out_shape=jax.ShapeDtypeStruct((batch, seq, LANES), F32),
        grid=(batch,),
        in_specs=[pl.BlockSpec((1, seq, d), lambda b: (b, 0, 0)),
                  pl.BlockSpec((d, LANES), lambda b: (0, 0)),
                  pl.BlockSpec((1, LANES), lambda b: (0, 0))],
        out_specs=pl.BlockSpec((1, seq, LANES), lambda b: (b, 0, 0)),
        compiler_params=_cparams("parallel"),
        name="fox_decay_cumsum",
    )(x3, wf, bfv)


def _split3_bf16(v):
    hi = v.astype(BF16).astype(F32)
    r = v - hi
    mid = r.astype(BF16).astype(F32)
    lo = (r - mid).astype(BF16).astype(F32)
    return hi, mid, lo


def _fox_kernel(q_ref, k_ref, v_ref, ccol_ref, o_ref, spec_ref, *, tq, nq, pps):
    hp = pl.program_id(1)
    qi = pl.program_id(2)
    n_heads = 2 * pps
    ones0 = 3 * n_heads
    lane = lax.broadcasted_iota(jnp.int32, (1, LANES), 1)
    lo_half = lane < HEAD_DIM
    heads = [(pr, e) for pr in range(pps) for e in range(2)]

    @pl.when(qi == 0)
    def _():
        parts = jnp.concatenate(_split3_bf16(-ccol_ref[0]), axis=1).astype(BF16)
        r = lax.broadcasted_iota(jnp.int32, (3 * LANES, LANES), 0)
        c = lax.broadcasted_iota(jnp.int32, (3 * LANES, LANES), 1)
        head_i = r % LANES - hp * n_heads
        pick = (head_i >= 0) & (head_i < n_heads) & (c == 3 * head_i + r // LANES)
        sel = jnp.where(pick, 1.0, 0.0).astype(BF16)
        spec = jnp.dot(parts, sel, preferred_element_type=F32)
        spec = jnp.where((lane >= ones0) & (lane < ones0 + 3), 1.0, spec)
        spec_ref[...] = spec.astype(BF16)

    ct_all = ccol_ref[0, pl.ds(pl.multiple_of(qi * tq, tq), tq), :]
    lane_c = lax.broadcasted_iota(jnp.int32, ct_all.shape, 1)
    q_heads = []
    for i, (pr, e) in enumerate(heads):
        qp = q_ref[0, :, pr * LANES:(pr + 1) * LANES]
        own = lo_half if e == 0 else jnp.logical_not(lo_half)
        qa = jnp.where(own, qp, jnp.zeros_like(qp))
        ct = jnp.sum(jnp.where(lane_c == hp * n_heads + i, ct_all, 0.0), axis=1, keepdims=True)
        hi, mid, lo = _split3_bf16(ct)
        qs = jnp.where((lane >= 3 * i) & (lane < 3 * i + 3), 1.0, 0.0)
        qs = jnp.where(lane == ones0, hi, qs)
        qs = jnp.where(lane == ones0 + 1, mid, qs)
        qs = jnp.where(lane == ones0 + 2, lo, qs)
        q_heads.append(jnp.concatenate([qa, qs.astype(BF16)], axis=1))
    rr = lax.broadcasted_iota(jnp.int32, (tq, tq), 0)
    cc_i = lax.broadcasted_iota(jnp.int32, (tq, tq), 1)
    causal = rr >= cc_i
    nchunk = tq // LANES

    def chunks(a):
        return [a[:, i * LANES:(i + 1) * LANES] for i in range(nchunk)]

    def attend(n):
        outs = []
        for idx, (pr, e) in enumerate(heads):
            m = None
            l_part = jnp.zeros((tq, LANES), F32)
            o = jnp.zeros((tq, LANES), F32)
            for j in range(n + 1):
                rows = slice(j * tq, (j + 1) * tq)
                kj = jnp.concatenate([k_ref[0, rows, pr * LANES:(pr + 1) * LANES],
                                      spec_ref[rows, :]], axis=1)
                s = lax.dot_general(q_heads[idx], kj, (((1,), (1,)), ((), ())),
                                    preferred_element_type=F32)
                if j == n:
                    s = jnp.where(causal, s, NEG)
                bm = jnp.max(functools.reduce(jnp.maximum, chunks(s)), axis=1, keepdims=True)
                if m is None:
                    m = bm
                else:
                    m_new = jnp.maximum(m, bm)
                    a = jnp.exp2(m - m_new)
                    l_part = l_part * a
                    o = o * a
                    m = m_new
                p = jnp.exp2(s - m)
                l_part = l_part + functools.reduce(lambda a, b: a + b, chunks(p))
                o = o + jnp.dot(p.astype(BF16), v_ref[0, rows, pr * LANES:(pr + 1) * LANES],
                                preferred_element_type=F32)
            outs.append(o / jnp.sum(l_part, axis=1, keepdims=True))
        for pr in range(pps):
            o_ref[0, :, pr * LANES:(pr + 1) * LANES] = jnp.where(
                lo_half, outs[2 * pr], outs[2 * pr + 1]).astype(o_ref.dtype)

    lax.switch(qi, [functools.partial(attend, n) for n in range(nq)])


FOX_PAIRS_PER_STEP = 4


def _fox_attention(proj, ccol, batch, seq):
    tq = min(256, seq)
    nq = seq // tq
    hd_all = FOX_HEADS * HEAD_DIM
    pps = FOX_PAIRS_PER_STEP
    width = pps * LANES
    groups = hd_all // width
    kern = functools.partial(_fox_kernel, tq=tq, nq=nq, pps=pps)
    return pl.pallas_call(
        kern,
        out_shape=jax.ShapeDtypeStruct((batch, seq, hd_all), BF16),
        grid=(batch, groups, nq),
        in_specs=[pl.BlockSpec((1, tq, width), lambda b, h, i: (b, i, h)),
                  pl.BlockSpec((1, seq, width), lambda b, h, i: (b, 0, groups + h)),
                  pl.BlockSpec((1, seq, width), lambda b, h, i: (b, 0, 2 * groups + h)),
                  pl.BlockSpec((1, seq, LANES), lambda b, h, i: (b, 0, 0))],
        out_specs=pl.BlockSpec((1, tq, width), lambda b, h, i: (b, i, h)),
        scratch_shapes=[pltpu.VMEM((seq, LANES), BF16)],
        compiler_params=_cparams("parallel", "parallel", "arbitrary"),
        name="fox_attention",
    )(proj, proj, proj, ccol)


def _layer_norm(y, gain, bias):
    mu = jnp.mean(y, axis=-1, keepdims=True)
    d = y - mu
    var = jnp.mean(d * d, axis=-1, keepdims=True)
    return d * lax.rsqrt(var + LN_EPS) * gain + bias


def _split_bf16(v):
    hi = v.astype(BF16)
    lo = (v - hi.astype(F32)).astype(BF16)
    return hi, lo


def _route(logits_t):
    rows = [logits_t[e:e + 1, :] for e in range(N_EXPERTS)]
    mx = functools.reduce(jnp.maximum, rows)
    ex = [jnp.exp(v - mx) for v in rows]
    den = functools.reduce(lambda a, b: a + b, ex)
    sc = [v / den for v in ex]
    gscore = []
    for g in range(N_GROUPS):
        mem = sc[g * EXPERTS_PER_GROUP:(g + 1) * EXPERTS_PER_GROUP]
        pairs = [mem[i] + mem[j] for i in range(4) for j in range(i + 1, 4)]
        gscore.append(functools.reduce(jnp.maximum, pairs))
    gmax = functools.reduce(jnp.maximum, gscore)
    taken = None
    gates = []
    for g in range(N_GROUPS):
        eq = gscore[g] == gmax
        best = eq if taken is None else eq & jnp.logical_not(taken)
        taken = eq if taken is None else taken | eq
        mem = sc[g * EXPERTS_PER_GROUP:(g + 1) * EXPERTS_PER_GROUP]
        sel = []
        for i in range(4):
            rank = jnp.zeros_like(mem[i])
            for j in range(4):
                if j == i:
                    continue
                ahead = (mem[j] > mem[i]) | ((mem[j] == mem[i]) & (j < i))
                rank = rank + jnp.where(ahead, 1.0, 0.0)
            sel.append(best & (rank < 2.0))
        tot = functools.reduce(lambda a, b: a + b,
                               [jnp.where(sel[i], mem[i], 0.0) for i in range(4)])
        for i in range(4):
            gates.append(jnp.where(sel[i], mem[i] / tot, 0.0))
    return gates


OUTPROJ_SUBTILES = 2


def _outproj_kernel(a_ref, wo_ref, x_ref, g_ref, b_ref, wr_ref, br_ref, x1_ref, gate_ref):
    sub = x_ref.shape[0] // OUTPROJ_SUBTILES
    wo = wo_ref[...].astype(BF16)
    for h in range(OUTPROJ_SUBTILES):
        rows = slice(h * sub, (h + 1) * sub)
        mix = jnp.dot(a_ref[rows, :], wo, preferred_element_type=F32)
        x1 = _layer_norm(ALPHA * x_ref[rows, :] + mix, g_ref[...], b_ref[...])
        x1_ref[rows, :] = x1
        hi, lo = _split_bf16(x1)
        from_hi = jnp.dot(hi, wr_ref[...], preferred_element_type=F32)
        from_lo = jnp.dot(lo, wr_ref[...], preferred_element_type=F32)
        logits = (from_hi[:, :LANES] + (from_lo[:, :LANES] + from_hi[:, LANES:])) + br_ref[...]
        routed = _route(logits.T[:N_EXPERTS, :])
        pad = jnp.zeros((LANES - len(routed), sub), F32)
        gate_ref[rows, :] = jnp.concatenate(list(routed) + [pad], axis=0).T


def _outproj_ln_router(a2, wo_stack, layer, x2, gain, bias, wr, br):
    t_tokens, d = x2.shape
    tm = min(1024, t_tokens)
    row = lambda i: (i, 0)
    fixed = lambda i: (0, 0)
    return pl.pallas_call(
        _outproj_kernel,
        out_shape=(jax.ShapeDtypeStruct((t_tokens, d), F32),
                   jax.ShapeDtypeStruct((t_tokens, LANES), F32)),
        grid=(t_tokens // tm,),
        in_specs=[pl.BlockSpec((tm, d), row),
                  pl.BlockSpec((None, d, d), lambda i: (layer, 0, 0)),
                  pl.BlockSpec((tm, d), row),
                  pl.BlockSpec((1, d), fixed),
                  pl.BlockSpec((1, d), fixed),
                  pl.BlockSpec((d, 2 * LANES), fixed),
                  pl.BlockSpec((1, LANES), fixed)],
        out_specs=(pl.BlockSpec((tm, d), row), pl.BlockSpec((tm, LANES), row)),
        compiler_params=_cparams("parallel"),
        name="outproj_ln_router",
    )(a2, wo_stack, x2, gain, bias, wr, br)


def _moe_kernel(x_ref, gate_ref, wg_ref, wu_ref, wd_ref, g_ref, b_ref, o_ref, xb_ref, acc_ref):
    grp = pl.program_id(1)

    @pl.when(grp == 0)
    def _():
        xb_ref[...] = x_ref[...].astype(BF16)
        acc_ref[...] = jnp.zeros_like(acc_ref)

    gates = gate_ref[...]
    lane = lax.broadcasted_iota(jnp.int32, gates.shape, 1)
    for k in range(EXPERTS_PER_GROUP):
        xb = xb_ref[...]
        hg = jnp.dot(xb, wg_ref[k].astype(BF16), preferred_element_type=F32)
        hu = jnp.dot(xb, wu_ref[k].astype(BF16), preferred_element_type=F32)
        ge = jnp.sum(jnp.where(lane == grp * EXPERTS_PER_GROUP + k, gates, 0.0),
                     axis=1, keepdims=True)
        act = hg * jax.nn.sigmoid(hg) * hu * ge
        acc_ref[...] += jnp.dot(act.astype(BF16), wd_ref[k].astype(BF16),
                                preferred_element_type=F32)

    @pl.when(grp == N_GROUPS - 1)
    def _():
        o_ref[...] = _layer_norm(ALPHA * x_ref[...] + acc_ref[...], g_ref[...], b_ref[...])


def _moe_ln(x2, gates, w_gate, w_up, w_down, layer, gain, bias):
    t_tokens, d = x2.shape
    tm = min(1024, t_tokens)
    row = lambda i, e: (i, 0)
    fixed = lambda i, e: (0, 0)
    per_group = lambda i, e: (layer, e, 0, 0)
    return pl.pallas_call(
        _moe_kernel,
        out_shape=jax.ShapeDtypeStruct((t_tokens, d), F32),
        grid=(t_tokens // tm, N_GROUPS),
        in_specs=[pl.BlockSpec((tm, d), row),
                  pl.BlockSpec((tm, LANES), row),
                  pl.BlockSpec((None, EXPERTS_PER_GROUP, d, D_EXPERT), per_group),
                  pl.BlockSpec((None, EXPERTS_PER_GROUP, d, D_EXPERT), per_group),
                  pl.BlockSpec((None, EXPERTS_PER_GROUP, D_EXPERT, d), per_group),
                  pl.BlockSpec((1, d), fixed),
                  pl.BlockSpec((1, d), fixed)],
        out_specs=pl.BlockSpec((tm, d), row),
        scratch_shapes=[pltpu.VMEM((tm, d), BF16), pltpu.VMEM((tm, d), F32)],
        compiler_params=_cparams("parallel", "arbitrary"),
        name="moe_ln",
    )(x2, gates, w_gate, w_up, w_down, gain, bias)


def _pad_lanes(w):
    return jnp.pad(w, ((0, 0), (0, LANES - w.shape[1])))


def kernel(x, ln_gain, ln_bias, swa_w_in, swa_sinks, swa_w_o, fox_w_in, fox_b_f, fox_w_o,
           w_router, b_router, w_gate, w_up, w_down):
    batch, seq, d = x.shape
    t_tokens = batch * seq
    tables = _rope_tables(seq)
    wr = jnp.concatenate(_split_bf16(_pad_lanes(w_router)), axis=1)
    br = _pad_lanes(b_router[None, :])
    fox_qkv = 3 * FOX_HEADS * HEAD_DIM
    x2 = x.reshape(t_tokens, d)
    for i in range(DEPTH):
        j = i // 2
        if i % 2 == 0:
            proj = _in_projection(x2, swa_w_in, j, swa_w_in.shape[2], seq,
                                  q_cols=SWA_Q_HEADS * HEAD_DIM, q_scale=QK_SCALE * LOG2E,
                                  rope_cols=(SWA_Q_HEADS + SWA_KV_HEADS) * HEAD_DIM,
                                  tables=tables)
            attn = _swa_attention(proj.reshape(batch, seq, -1), swa_sinks[j] * LOG2E, batch, seq)
            w_o = swa_w_o
        else:
            proj = _in_projection(x2, fox_w_in, j, fox_qkv, seq,
                                  q_cols=FOX_HEADS * HEAD_DIM, q_scale=QK_SCALE * LOG2E,
                                  rope_cols=0, tables=tables)
            ccol = _decay_cumsum(x2.reshape(batch, seq, d),
                                 _pad_lanes(fox_w_in[j, :, fox_qkv:]).astype(BF16),
                                 _pad_lanes(fox_b_f[j][None, :]))
            attn = _fox_attention(proj.reshape(batch, seq, -1), ccol, batch, seq)
            w_o = fox_w_o
        x1, gates = _outproj_ln_router(attn.reshape(t_tokens, d), w_o, j, x2,
                                       ln_gain[i, 0][None, :], ln_bias[i, 0][None, :],
                                       wr, br)
        x2 = _moe_ln(x1, gates, w_gate, w_up, w_down, i,
                     ln_gain[i, 1][None, :], ln_bias[i, 1][None, :])
    return x2.reshape(batch, seq, d)
```

```python
import functools
import math

import jax
import jax.numpy as jnp
from jax import lax
from jax.experimental import pallas as pl
from jax.experimental.pallas import tpu as pltpu

F32 = jnp.float32
BF16 = jnp.bfloat16

HEAD_DIM = 64
ROT_DIM = 16
ROT_HALF = ROT_DIM // 2
ROPE_THETA = 500000.0
SWA_Q_HEADS = 16
SWA_KV_HEADS = 4
SWA_BLOCK = 128
FOX_HEADS = 16
N_EXPERTS = 16
N_GROUPS = 4
EXPERTS_PER_GROUP = 4
D_EXPERT = 256
DEPTH = 4
ALPHA = (2.0 * DEPTH) ** 0.25
LN_EPS = 1e-5
NEG = -1e30
QK_SCALE = HEAD_DIM ** -0.5
LOG2E = math.log2(math.e)

LANES = 128
VMEM_LIMIT = 56 * 1024 * 1024


def _cparams(*sem):
    return pltpu.CompilerParams(dimension_semantics=sem, vmem_limit_bytes=VMEM_LIMIT)


def _inproj_kernel(x_ref, w_ref, cos_ref, sa_ref, sb_ref, o_ref, *, tn, q_cols, q_scale,
                   rope_cols):
    xb = x_ref[...].astype(BF16)
    n = o_ref.shape[1]
    for c0 in range(0, n, tn):
        acc = jnp.dot(xb, w_ref[:, c0:c0 + tn].astype(BF16), preferred_element_type=F32)
        n_rope = min(max(rope_cols - c0, 0), tn)
        if n_rope:
            reps = n_rope // LANES
            head = acc[:, :n_rope]
            head = (head * jnp.tile(cos_ref[...], (1, reps))
                    + pltpu.roll(head, ROT_HALF, 1) * jnp.tile(sa_ref[...], (1, reps))
                    + pltpu.roll(head, n_rope - ROT_HALF, 1) * jnp.tile(sb_ref[...], (1, reps)))
            acc = head if n_rope == tn else jnp.concatenate([head, acc[:, n_rope:]], axis=1)
        if c0 + tn <= q_cols:
            acc = acc * q_scale
        else:
            assert c0 >= q_cols, "q columns must end on a chunk boundary"
        o_ref[:, c0:c0 + tn] = acc.astype(o_ref.dtype)


def _rope_tables(seq):
    inv_freq = jnp.power(ROPE_THETA, -jnp.arange(ROT_HALF, dtype=F32) * (2.0 / ROT_DIM))
    ang = jnp.arange(seq, dtype=F32)[:, None] * inv_freq[None, :]
    cos, sin = jnp.cos(ang), jnp.sin(ang)
    ones = jnp.ones((seq, HEAD_DIM - ROT_DIM), F32)
    zeros = jnp.zeros((seq, HEAD_DIM - ROT_DIM), F32)
    z8 = jnp.zeros((seq, ROT_HALF), F32)
    cos_h = jnp.concatenate([cos, cos, ones], axis=1)
    sa_h = jnp.concatenate([z8, sin, zeros], axis=1)
    sb_h = jnp.concatenate([-sin, z8, zeros], axis=1)
    rep = LANES // HEAD_DIM
    return tuple(jnp.tile(t, (1, rep)) for t in (cos_h, sa_h, sb_h))


def _in_projection(x2, w_stack, layer, n, seq, *, q_cols, q_scale, rope_cols, tables):
    t_tokens, d = x2.shape
    tm = min(1024, seq)
    tn = 512
    assert t_tokens % tm == 0 and seq % tm == 0 and n % tn == 0
    pos_blocks = seq // tm
    kern = functools.partial(_inproj_kernel, tn=tn, q_cols=q_cols, q_scale=q_scale,
                             rope_cols=rope_cols)
    tab_spec = pl.BlockSpec((tm, LANES), lambda i: (i % pos_blocks, 0))
    return pl.pallas_call(
        kern,
        out_shape=jax.ShapeDtypeStruct((t_tokens, n), BF16),
        grid=(t_tokens // tm,),
        in_specs=[pl.BlockSpec((tm, d), lambda i: (i, 0)),
                  pl.BlockSpec((None,) + w_stack.shape[1:], lambda i: (layer, 0, 0),
                               pipeline_mode=pl.Buffered(1)),
                  tab_spec, tab_spec, tab_spec],
        out_specs=pl.BlockSpec((tm, n), lambda i: (i, 0)),
        compiler_params=_cparams("parallel"),
        name="in_projection",
    )(x2, w_stack, *tables)


def _swa_kernel(sink_ref, q_ref, kp_ref, kc_ref, vp_ref, vc_ref, o_ref):
    n = pl.program_id(1)
    blk = SWA_BLOCK
    lane = lax.broadcasted_iota(jnp.int32, (1, LANES), 1)
    lo = lane < HEAD_DIM
    r = lax.broadcasted_iota(jnp.int32, (blk, 2 * blk), 0)
    c = lax.broadcasted_iota(jnp.int32, (blk, 2 * blk), 1)
    allowed = (c > r) & (c <= r + blk) & ((c >= blk) | (n > 0))
    k_all = jnp.concatenate([kp_ref[0], kc_ref[0]], axis=0).astype(F32)
    v_all = jnp.concatenate([vp_ref[0], vc_ref[0]], axis=0).astype(F32)
    for g in range(SWA_KV_HEADS):
        grp = g // 2
        kf = k_all[:, grp * LANES:(grp + 1) * LANES]
        vf = v_all[:, grp * LANES:(grp + 1) * LANES]
        kr = pltpu.roll(kf, HEAD_DIM, 1)
        vr = pltpu.roll(vf, HEAD_DIM, 1)
        own_lo = (g % 2 == 0)
        k_src_lo, k_src_hi = (kf, kr) if own_lo else (kr, kf)
        k_lo = jnp.where(lo, k_src_lo, 0.0).astype(BF16)
        k_hi = jnp.where(lo, 0.0, k_src_hi).astype(BF16)
        v_lo = jnp.where(lo, vf if own_lo else vr, 1.0).astype(BF16)
        v_hi = jnp.where(lo, 1.0, vr if own_lo else vf).astype(BF16)
        for pp in range(2):
            p = 2 * g + pp
            qp = q_ref[0, :, p * LANES:(p + 1) * LANES]
            outs, sink_terms = [], []
            for e, (kk, vv) in enumerate(((k_lo, v_lo), (k_hi, v_hi))):
                s = lax.dot_general(qp, kk, (((1,), (1,)), ((), ())),
                                    preferred_element_type=F32)
                s = jnp.where(allowed, s, NEG)
                sink = sink_ref[2 * p + e]
                m = jnp.maximum(jnp.max(s, axis=1, keepdims=True), sink)
                pr = jnp.exp2(s - m)
                outs.append(jnp.dot(pr.astype(BF16), vv, preferred_element_type=F32))
                sink_terms.append(jnp.exp2(sink - m))
            num = jnp.where(lo, outs[0], outs[1])
            den = (pltpu.roll(jnp.where(lo, outs[1], outs[0]), HEAD_DIM, 1)
                   + jnp.where(lo, sink_terms[0], sink_terms[1]))
            o_ref[0, :, p * LANES:(p + 1) * LANES] = (num / den).astype(o_ref.dtype)


def _swa_attention(proj, sinks, batch, seq):
    blk = SWA_BLOCK
    nb = seq // blk
    qd = SWA_Q_HEADS * HEAD_DIM
    kd = SWA_KV_HEADS * HEAD_DIM
    kcol, vcol = qd // kd, qd // kd + 1
    prev = lambda b, n: jnp.maximum(n - 1, 0)
    return pl.pallas_call(
        _swa_kernel,
        out_shape=jax.ShapeDtypeStruct((batch, seq, qd), BF16),
        grid=(batch, nb),
        in_specs=[pl.BlockSpec(memory_space=pltpu.SMEM),
                  pl.BlockSpec((1, blk, qd), lambda b, n: (b, n, 0)),
                  pl.BlockSpec((1, blk, kd), lambda b, n: (b, prev(b, n), kcol)),
                  pl.BlockSpec((1, blk, kd), lambda b, n: (b, n, kcol)),
                  pl.BlockSpec((1, blk, kd), lambda b, n: (b, prev(b, n), vcol)),
                  pl.BlockSpec((1, blk, kd), lambda b, n: (b, n, vcol))],
        out_specs=pl.BlockSpec((1, blk, qd), lambda b, n: (b, n, 0)),
        compiler_params=_cparams("parallel", "arbitrary"),
        name="swa_attention",
    )(sinks, proj, proj, proj, proj, proj)


def _decay_kernel(x_ref, wf_ref, bf_ref, ccol_ref):
    seq = x_ref.shape[1]
    f = jnp.dot(x_ref[0].astype(BF16), wf_ref[...], preferred_element_type=F32) + bf_ref[...]
    ls = jnp.minimum(f, 0.0) - jnp.log1p(jnp.exp(-jnp.abs(f)))
    row = lax.broadcasted_iota(jnp.int32, ls.shape, 0)
    c = ls
    k = 1
    while k < seq:
        c = c + jnp.where(row >= k, pltpu.roll(c, k, 0), 0.0)
        k *= 2
    ccol_ref[0] = c * LOG2E


def _decay_cumsum(x3, wf, bfv):
    batch, seq, d = x3.shape
    return pl.pallas_call(
        _decay_kernel,
        out_shape=jax.ShapeDtypeStruct((batch, seq, LANES), F32),
        grid=(batch,),
        in_specs=[pl.BlockSpec((1, seq, d), lambda b: (b, 0, 0)),
                  pl.BlockSpec((d, LANES), lambda b: (0, 0)),
                  pl.BlockSpec((1, LANES), lambda b: (0, 0))],
        out_specs=pl.BlockSpec((1, seq, LANES), lambda b: (b, 0, 0)),
        compiler_params=_cparams("parallel"),
        name="fox_decay_cumsum",
    )(x3, wf, bfv)


def _split3_bf16(v):
    hi = v.astype(BF16).astype(F32)
    r = v - hi
    mid = r.astype(BF16).astype(F32)
    lo = (r - mid).astype(BF16).astype(F32)
    return hi, mid, lo


def _fox_kernel(q_ref, k_ref, v_ref, ccol_ref, o_ref, spec_ref, *, tq, nq, pps):
    hp = pl.program_id(1)
    qi = pl.program_id(2)
    n_heads = 2 * pps
    ones0 = 3 * n_heads
    lane = lax.broadcasted_iota(jnp.int32, (1, LANES), 1)
    lo_half = lane < HEAD_DIM
    heads = [(pr, e) for pr in range(pps) for e in range(2)]

    @pl.when(qi == 0)
    def _():
        parts = jnp.concatenate(_split3_bf16(-ccol_ref[0]), axis=1).astype(BF16)
        r = lax.broadcasted_iota(jnp.int32, (3 * LANES, LANES), 0)
        c = lax.broadcasted_iota(jnp.int32, (3 * LANES, LANES), 1)
        head_i = r % LANES - hp * n_heads
        pick = (head_i >= 0) & (head_i < n_heads) & (c == 3 * head_i + r // LANES)
        sel = jnp.where(pick, 1.0, 0.0).astype(BF16)
        spec = jnp.dot(parts, sel, preferred_element_type=F32)
        spec = jnp.where((lane >= ones0) & (lane < ones0 + 3), 1.0, spec)
        spec_ref[...] = spec.astype(BF16)

    ct_all = ccol_ref[0, pl.ds(pl.multiple_of(qi * tq, tq), tq), :]
    lane_c = lax.broadcasted_iota(jnp.int32, ct_all.shape, 1)
    q_heads = []
    for i, (pr, e) in enumerate(heads):
        qp = q_ref[0, :, pr * LANES:(pr + 1) * LANES]
        own = lo_half if e == 0 else jnp.logical_not(lo_half)
        qa = jnp.where(own, qp, jnp.zeros_like(qp))
        ct = jnp.sum(jnp.where(lane_c == hp * n_heads + i, ct_all, 0.0), axis=1, keepdims=True)
        hi, mid, lo = _split3_bf16(ct)
        qs = jnp.where((lane >= 3 * i) & (lane < 3 * i + 3), 1.0, 0.0)
        qs = jnp.where(lane == ones0, hi, qs)
        qs = jnp.where(lane == ones0 + 1, mid, qs)
        qs = jnp.where(lane == ones0 + 2, lo, qs)
        q_heads.append(jnp.concatenate([qa, qs.astype(BF16)], axis=1))
    rr = lax.broadcasted_iota(jnp.int32, (tq, tq), 0)
    cc_i = lax.broadcasted_iota(jnp.int32, (tq, tq), 1)
    causal = rr >= cc_i
    nchunk = tq // LANES

    def chunks(a):
        return [a[:, i * LANES:(i + 1) * LANES] for i in range(nchunk)]

    def attend(n):
        outs = []
        for idx, (pr, e) in enumerate(heads):
            m = None
            l_part = jnp.zeros((tq, LANES), F32)
            o = jnp.zeros((tq, LANES), F32)
            for j in range(n + 1):
                rows = slice(j * tq, (j + 1) * tq)
                kj = jnp.concatenate([k_ref[0, rows, pr * LANES:(pr + 1) * LANES],
                                      spec_ref[rows, :]], axis=1)
                s = lax.dot_general(q_heads[idx], kj, (((1,), (1,)), ((), ())),
                                    preferred_element_type=F32)
                if j == n:
                    s = jnp.where(causal, s, NEG)
                bm = jnp.max(functools.reduce(jnp.maximum, chunks(s)), axis=1, keepdims=True)
                if m is None:
                    m = bm
                else:
                    m_new = jnp.maximum(m, bm)
                    a = jnp.exp2(m - m_new)
                    l_part = l_part * a
                    o = o * a
                    m = m_new
                p = jnp.exp2(s - m)
                l_part = l_part + functools.reduce(lambda a, b: a + b, chunks(p))
                o = o + jnp.dot(p.astype(BF16), v_ref[0, rows, pr * LANES:(pr + 1) * LANES],
                                preferred_element_type=F32)
            outs.append(o / jnp.sum(l_part, axis=1, keepdims=True))
        for pr in range(pps):
            o_ref[0, :, pr * LANES:(pr + 1) * LANES] = jnp.where(
                lo_half, outs[2 * pr], outs[2 * pr + 1]).astype(o_ref.dtype)

    lax.switch(qi, [functools.partial(attend, n) for n in range(nq)])


FOX_PAIRS_PER_STEP = 4


def _fox_attention(proj, ccol, batch, seq):
    tq = min(256, seq)
    nq = seq // tq
    hd_all = FOX_HEADS * HEAD_DIM
    pps = FOX_PAIRS_PER_STEP
    width = pps * LANES
    groups = hd_all // width
    kern = functools.partial(_fox_kernel, tq=tq, nq=nq, pps=pps)
    return pl.pallas_call(
        kern,
        out_shape=jax.ShapeDtypeStruct((batch, seq, hd_all), BF16),
        grid=(batch, groups, nq),
        in_specs=[pl.BlockSpec((1, tq, width), lambda b, h, i: (b, i, h)),
                  pl.BlockSpec((1, seq, width), lambda b, h, i: (b, 0, groups + h)),
                  pl.BlockSpec((1, seq, width), lambda b, h, i: (b, 0, 2 * groups + h)),
                  pl.BlockSpec((1, seq, LANES), lambda b, h, i: (b, 0, 0))],
        out_specs=pl.BlockSpec((1, tq, width), lambda b, h, i: (b, i, h)),
        scratch_shapes=[pltpu.VMEM((seq, LANES), BF16)],
        compiler_params=_cparams("parallel", "parallel", "arbitrary"),
        name="fox_attention",
    )(proj, proj, proj, ccol)


def _layer_norm(y, gain, bias):
    mu = jnp.mean(y, axis=-1, keepdims=True)
    d = y - mu
    var = jnp.mean(d * d, axis=-1, keepdims=True)
    return d * lax.rsqrt(var + LN_EPS) * gain + bias


def _split_bf16(v):
    hi = v.astype(BF16)
    lo = (v - hi.astype(F32)).astype(BF16)
    return hi, lo


def _route(logits_t):
    rows = [logits_t[e:e + 1, :] for e in range(N_EXPERTS)]
    mx = functools.reduce(jnp.maximum, rows)
    ex = [jnp.exp(v - mx) for v in rows]
    den = functools.reduce(lambda a, b: a + b, ex)
    sc = [v / den for v in ex]
    gscore = []
    for g in range(N_GROUPS):
        mem = sc[g * EXPERTS_PER_GROUP:(g + 1) * EXPERTS_PER_GROUP]
        pairs = [mem[i] + mem[j] for i in range(4) for j in range(i + 1, 4)]
        gscore.append(functools.reduce(jnp.maximum, pairs))
    gmax = functools.reduce(jnp.maximum, gscore)
    taken = None
    gates = []
    for g in range(N_GROUPS):
        eq = gscore[g] == gmax
        best = eq if taken is None else eq & jnp.logical_not(taken)
        taken = eq if taken is None else taken | eq
        mem = sc[g * EXPERTS_PER_GROUP:(g + 1) * EXPERTS_PER_GROUP]
        sel = []
        for i in range(4):
            rank = jnp.zeros_like(mem[i])
            for j in range(4):
                if j == i:
                    continue
                ahead = (mem[j] > mem[i]) | ((mem[j] == mem[i]) & (j < i))
                rank = rank + jnp.where(ahead, 1.0, 0.0)
            sel.append(best & (rank < 2.0))
        tot = functools.reduce(lambda a, b: a + b,
                               [jnp.where(sel[i], mem[i], 0.0) for i in range(4)])
        for i in range(4):
            gates.append(jnp.where(sel[i], mem[i] / tot, 0.0))
    return gates


OUTPROJ_SUBTILES = 2


def _outproj_kernel(a_ref, wo_ref, x_ref, g_ref, b_ref, wr_ref, br_ref, x1_ref, gate_ref):
    sub = x_ref.shape[0] // OUTPROJ_SUBTILES
    wo = wo_ref[...].astype(BF16)
    for h in range(OUTPROJ_SUBTILES):
        rows = slice(h * sub, (h + 1) * sub)
        mix = jnp.dot(a_ref[rows, :], wo, preferred_element_type=F32)
        x1 = _layer_norm(ALPHA * x_ref[rows, :] + mix, g_ref[...], b_ref[...])
        x1_ref[rows, :] = x1
        hi, lo = _split_bf16(x1)
        from_hi = jnp.dot(hi, wr_ref[...], preferred_element_type=F32)
        from_lo = jnp.dot(lo, wr_ref[...], preferred_element_type=F32)
        logits = (from_hi[:, :LANES] + (from_lo[:, :LANES] + from_hi[:, LANES:])) + br_ref[...]
        routed = _route(logits.T[:N_EXPERTS, :])
        pad = jnp.zeros((LANES - len(routed), sub), F32)
        gate_ref[rows, :] = jnp.concatenate(list(routed) + [pad], axis=0).T


def _outproj_ln_router(a2, wo_stack, layer, x2, gain, bias, wr, br):
    t_tokens, d = x2.shape
    tm = min(1024, t_tokens)
    row = lambda i: (i, 0)
    fixed = lambda i: (0, 0)
    return pl.pallas_call(
        _outproj_kernel,
        out_shape=(jax.ShapeDtypeStruct((t_tokens, d), F32),
                   jax.ShapeDtypeStruct((t_tokens, LANES), F32)),
        grid=(t_tokens // tm,),
        in_specs=[pl.BlockSpec((tm, d), row),
                  pl.BlockSpec((None, d, d), lambda i: (layer, 0, 0)),
                  pl.BlockSpec((tm, d), row),
                  pl.BlockSpec((1, d), fixed),
                  pl.BlockSpec((1, d), fixed),
                  pl.BlockSpec((d, 2 * LANES), fixed),
                  pl.BlockSpec((1, LANES), fixed)],
        out_specs=(pl.BlockSpec((tm, d), row), pl.BlockSpec((tm, LANES), row)),
        compiler_params=_cparams("parallel"),
        name="outproj_ln_router",
    )(a2, wo_stack, x2, gain, bias, wr, br)


def _moe_kernel(x_ref, gate_ref, wg_ref, wu_ref, wd_ref, g_ref, b_ref, o_ref, xb_ref, acc_ref):
    grp = pl.program_id(1)

    @pl.when(grp == 0)
    def _():
        xb_ref[...] = x_ref[...].astype(BF16)
        acc_ref[...] = jnp.zeros_like(acc_ref)

    gates = gate_ref[...]
    lane = lax.broadcasted_iota(jnp.int32, gates.shape, 1)
    acts = []
    for k in range(EXPERTS_PER_GROUP):
        xb = xb_ref[...]
        hg = jnp.dot(xb, wg_ref[k].astype(BF16), preferred_element_type=F32)
        hu = jnp.dot(xb, wu_ref[k].astype(BF16), preferred_element_type=F32)
        ge = jnp.sum(jnp.where(lane == grp * EXPERTS_PER_GROUP + k, gates, 0.0),
                     axis=1, keepdims=True)
        acts.append((hg * jax.nn.sigmoid(hg) * hu * ge).astype(BF16))
    wd_all = wd_ref[...].astype(BF16).reshape(EXPERTS_PER_GROUP * D_EXPERT, wd_ref.shape[2])
    acc_ref[...] += jnp.dot(jnp.concatenate(acts, axis=1), wd_all, preferred_element_type=F32)

    @pl.when(grp == N_GROUPS - 1)
    def _():
        o_ref[...] = _layer_norm(ALPHA * x_ref[...] + acc_ref[...], g_ref[...], b_ref[...])


def _moe_ln(x2, gates, w_gate, w_up, w_down, layer, gain, bias):
    t_tokens, d = x2.shape
    tm = min(1024, t_tokens)
    row = lambda i, e: (i, 0)
    fixed = lambda i, e: (0, 0)
    per_group = lambda i, e: (layer, e, 0, 0)
    return pl.pallas_call(
        _moe_kernel,
        out_shape=jax.ShapeDtypeStruct((t_tokens, d), F32),
        grid=(t_tokens // tm, N_GROUPS),
        in_specs=[pl.BlockSpec((tm, d), row),
                  pl.BlockSpec((tm, LANES), row),
                  pl.BlockSpec((None, EXPERTS_PER_GROUP, d, D_EXPERT), per_group),
                  pl.BlockSpec((None, EXPERTS_PER_GROUP, d, D_EXPERT), per_group),
                  pl.BlockSpec((None, EXPERTS_PER_GROUP, D_EXPERT, d), per_group),
                  pl.BlockSpec((1, d), fixed),
                  pl.BlockSpec((1, d), fixed)],
        out_specs=pl.BlockSpec((tm, d), row),
        scratch_shapes=[pltpu.VMEM((tm, d), BF16), pltpu.VMEM((tm, d), F32)],
        compiler_params=_cparams("parallel", "arbitrary"),
        name="moe_ln",
    )(x2, gates, w_gate, w_up, w_down, gain, bias)


def _pad_lanes(w):
    return jnp.pad(w, ((0, 0), (0, LANES - w.shape[1])))


def kernel(x, ln_gain, ln_bias, swa_w_in, swa_sinks, swa_w_o, fox_w_in, fox_b_f, fox_w_o,
           w_router, b_router, w_gate, w_up, w_down):
    batch, seq, d = x.shape
    t_tokens = batch * seq
    tables = _rope_tables(seq)
    wr = jnp.concatenate(_split_bf16(_pad_lanes(w_router)), axis=1)
    br = _pad_lanes(b_router[None, :])
    fox_qkv = 3 * FOX_HEADS * HEAD_DIM
    x2 = x.reshape(t_tokens, d)
    for i in range(DEPTH):
        j = i // 2
        if i % 2 == 0:
            proj = _in_projection(x2, swa_w_in, j, swa_w_in.shape[2], seq,
                                  q_cols=SWA_Q_HEADS * HEAD_DIM, q_scale=QK_SCALE * LOG2E,
                                  rope_cols=(SWA_Q_HEADS + SWA_KV_HEADS) * HEAD_DIM,
                                  tables=tables)
            attn = _swa_attention(proj.reshape(batch, seq, -1), swa_sinks[j] * LOG2E, batch, seq)
            w_o = swa_w_o
        else:
            proj = _in_projection(x2, fox_w_in, j, fox_qkv, seq,
                                  q_cols=FOX_HEADS * HEAD_DIM, q_scale=QK_SCALE * LOG2E,
                                  rope_cols=0, tables=tables)
            ccol = _decay_cumsum(x2.reshape(batch, seq, d),
                                 _pad_lanes(fox_w_in[j, :, fox_qkv:]).astype(BF16),
                                 _pad_lanes(fox_b_f[j][None, :]))
            attn = _fox_attention(proj.reshape(batch, seq, -1), ccol, batch, seq)
            w_o = fox_w_o
        x1, gates = _outproj_ln_router(attn.reshape(t_tokens, d), w_o, j, x2,
                                       ln_gain[i, 0][None, :], ln_bias[i, 0][None, :],
                                       wr, br)
        x2 = _moe_ln(x1, gates, w_gate, w_up, w_down, i,
                     ln_gain[i, 1][None, :], ln_bias[i, 1][None, :])
    return x2.reshape(batch, seq, d)
```

```python
import functools
import math

import jax
import jax.numpy as jnp
from jax import lax
from jax.experimental import pallas as pl
from jax.experimental.pallas import tpu as pltpu

F32 = jnp.float32
BF16 = jnp.bfloat16

HEAD_DIM = 64
ROT_DIM = 16
ROT_HALF = ROT_DIM // 2
ROPE_THETA = 500000.0
SWA_Q_HEADS = 16
SWA_KV_HEADS = 4
SWA_BLOCK = 128
FOX_HEADS = 16
N_EXPERTS = 16
N_GROUPS = 4
EXPERTS_PER_GROUP = 4
D_EXPERT = 256
DEPTH = 4
ALPHA = (2.0 * DEPTH) ** 0.25
LN_EPS = 1e-5
NEG = -1e30
QK_SCALE = HEAD_DIM ** -0.5
LOG2E = math.log2(math.e)

LANES = 128
VMEM_LIMIT = 56 * 1024 * 1024


def _cparams(*sem):
    return pltpu.CompilerParams(dimension_semantics=sem, vmem_limit_bytes=VMEM_LIMIT)


def _inproj_kernel(x_ref, w_ref, cos_ref, sa_ref, sb_ref, o_ref, *, tn, q_cols, q_scale,
                   rope_cols):
    xb = x_ref[...].astype(BF16)
    n = o_ref.shape[1]
    for c0 in range(0, n, tn):
        acc = jnp.dot(xb, w_ref[:, c0:c0 + tn].astype(BF16), preferred_element_type=F32)
        n_rope = min(max(rope_cols - c0, 0), tn)
        if n_rope:
            reps = n_rope // LANES
            head = acc[:, :n_rope]
            head = (head * jnp.tile(cos_ref[...], (1, reps))
                    + pltpu.roll(head, ROT_HALF, 1) * jnp.tile(sa_ref[...], (1, reps))
                    + pltpu.roll(head, n_rope - ROT_HALF, 1) * jnp.tile(sb_ref[...], (1, reps)))
            acc = head if n_rope == tn else jnp.concatenate([head, acc[:, n_rope:]], axis=1)
        if c0 + tn <= q_cols:
            acc = acc * q_scale
        else:
            assert c0 >= q_cols, "q columns must end on a chunk boundary"
        o_ref[:, c0:c0 + tn] = acc.astype(o_ref.dtype)


def _rope_tables(seq):
    inv_freq = jnp.power(ROPE_THETA, -jnp.arange(ROT_HALF, dtype=F32) * (2.0 / ROT_DIM))
    ang = jnp.arange(seq, dtype=F32)[:, None] * inv_freq[None, :]
    cos, sin = jnp.cos(ang), jnp.sin(ang)
    ones = jnp.ones((seq, HEAD_DIM - ROT_DIM), F32)
    zeros = jnp.zeros((seq, HEAD_DIM - ROT_DIM), F32)
    z8 = jnp.zeros((seq, ROT_HALF), F32)
    cos_h = jnp.concatenate([cos, cos, ones], axis=1)
    sa_h = jnp.concatenate([z8, sin, zeros], axis=1)
    sb_h = jnp.concatenate([-sin, z8, zeros], axis=1)
    rep = LANES // HEAD_DIM
    return tuple(jnp.tile(t, (1, rep)) for t in (cos_h, sa_h, sb_h))


def _in_projection(x2, w_stack, layer, n, seq, *, q_cols, q_scale, rope_cols, tables):
    t_tokens, d = x2.shape
    tm = min(1024, seq)
    tn = 512
    assert t_tokens % tm == 0 and seq % tm == 0 and n % tn == 0
    pos_blocks = seq // tm
    kern = functools.partial(_inproj_kernel, tn=tn, q_cols=q_cols, q_scale=q_scale,
                             rope_cols=rope_cols)
    tab_spec = pl.BlockSpec((tm, LANES), lambda i: (i % pos_blocks, 0))
    return pl.pallas_call(
        kern,
        out_shape=jax.ShapeDtypeStruct((t_tokens, n), BF16),
        grid=(t_tokens // tm,),
        in_specs=[pl.BlockSpec((tm, d), lambda i: (i, 0)),
                  pl.BlockSpec((None,) + w_stack.shape[1:], lambda i: (layer, 0, 0),
                               pipeline_mode=pl.Buffered(1)),
                  tab_spec, tab_spec, tab_spec],
        out_specs=pl.BlockSpec((tm, n), lambda i: (i, 0)),
        compiler_params=_cparams("parallel"),
        name="in_projection",
    )(x2, w_stack, *tables)


SWA_BLOCKS_PER_STEP = 4


def _swa_kernel(sink_ref, q_ref, kp_ref, kc_ref, vp_ref, vc_ref, o_ref):
    n = pl.program_id(1)
    blk = SWA_BLOCK
    nsub = q_ref.shape[1] // blk
    lane = lax.broadcasted_iota(jnp.int32, (1, LANES), 1)
    lo = lane < HEAD_DIM
    r = lax.broadcasted_iota(jnp.int32, (blk, 2 * blk), 0)
    c = lax.broadcasted_iota(jnp.int32, (blk, 2 * blk), 1)
    band = (c > r) & (c <= r + blk)
    allowed = [band & ((c >= blk) | (n > 0))] + [band] * (nsub - 1)
    k_all = jnp.concatenate([kp_ref[0], kc_ref[0]], axis=0).astype(F32)
    v_all = jnp.concatenate([vp_ref[0], vc_ref[0]], axis=0).astype(F32)
    for g in range(SWA_KV_HEADS):
        grp = g // 2
        kf = k_all[:, grp * LANES:(grp + 1) * LANES]
        vf = v_all[:, grp * LANES:(grp + 1) * LANES]
        kr = pltpu.roll(kf, HEAD_DIM, 1)
        vr = pltpu.roll(vf, HEAD_DIM, 1)
        own_lo = (g % 2 == 0)
        k_src_lo, k_src_hi = (kf, kr) if own_lo else (kr, kf)
        k_lo = jnp.where(lo, k_src_lo, 0.0).astype(BF16)
        k_hi = jnp.where(lo, 0.0, k_src_hi).astype(BF16)
        v_lo = jnp.where(lo, vf if own_lo else vr, 1.0).astype(BF16)
        v_hi = jnp.where(lo, 1.0, vr if own_lo else vf).astype(BF16)
        for sub in range(nsub):
            qrows = slice(sub * blk, (sub + 1) * blk)
            krows = slice(sub * blk, (sub + 2) * blk)
            for pp in range(2):
                p = 2 * g + pp
                qp = q_ref[0, qrows, p * LANES:(p + 1) * LANES]
                outs, sink_terms = [], []
                for e, (kk, vv) in enumerate(((k_lo, v_lo), (k_hi, v_hi))):
                    s = lax.dot_general(qp, kk[krows], (((1,), (1,)), ((), ())),
                                        preferred_element_type=F32)
                    s = jnp.where(allowed[sub], s, NEG)
                    sink = sink_ref[2 * p + e]
                    m = jnp.maximum(jnp.max(s, axis=1, keepdims=True), sink)
                    pr = jnp.exp2(s - m)
                    outs.append(jnp.dot(pr.astype(BF16), vv[krows], preferred_element_type=F32))
                    sink_terms.append(jnp.exp2(sink - m))
                num = jnp.where(lo, outs[0], outs[1])
                den = (pltpu.roll(jnp.where(lo, outs[1], outs[0]), HEAD_DIM, 1)
                       + jnp.where(lo, sink_terms[0], sink_terms[1]))
                o_ref[0, qrows, p * LANES:(p + 1) * LANES] = (num / den).astype(o_ref.dtype)


def _swa_attention(proj, sinks, batch, seq):
    blk = SWA_BLOCK
    nb = seq // blk
    qd = SWA_Q_HEADS * HEAD_DIM
    kd = SWA_KV_HEADS * HEAD_DIM
    kcol, vcol = qd // kd, qd // kd + 1
    nsub = min(SWA_BLOCKS_PER_STEP, nb)
    run = nsub * blk
    assert nb % nsub == 0
    prev = lambda b, n: jnp.maximum(n * nsub - 1, 0)
    return pl.pallas_call(
        _swa_kernel,
        out_shape=jax.ShapeDtypeStruct((batch, seq, qd), BF16),
        grid=(batch, nb // nsub),
        in_specs=[pl.BlockSpec(memory_space=pltpu.SMEM),
                  pl.BlockSpec((1, run, qd), lambda b, n: (b, n, 0)),
                  pl.BlockSpec((1, blk, kd), lambda b, n: (b, prev(b, n), kcol)),
                  pl.BlockSpec((1, run, kd), lambda b, n: (b, n, kcol)),
                  pl.BlockSpec((1, blk, kd), lambda b, n: (b, prev(b, n), vcol)),
                  pl.BlockSpec((1, run, kd), lambda b, n: (b, n, vcol))],
        out_specs=pl.BlockSpec((1, run, qd), lambda b, n: (b, n, 0)),
        compiler_params=_cparams("parallel", "arbitrary"),
        name="swa_attention",
    )(sinks, proj, proj, proj, proj, proj)


def _decay_kernel(x_ref, wf_ref, bf_ref, ccol_ref):
    seq = x_ref.shape[1]
    f = jnp.dot(x_ref[0].astype(BF16), wf_ref[...], preferred_element_type=F32) + bf_ref[...]
    ls = jnp.minimum(f, 0.0) - jnp.log1p(jnp.exp(-jnp.abs(f)))
    row = lax.broadcasted_iota(jnp.int32, ls.shape, 0)
    c = ls
    k = 1
    while k < seq:
        c = c + jnp.where(row >= k, pltpu.roll(c, k, 0), 0.0)
        k *= 2
    ccol_ref[0] = c * LOG2E


def _decay_cumsum(x3, wf, bfv):
    batch, seq, d = x3.shape
    return pl.pallas_call(
        _decay_kernel,
        out_shape=jax.ShapeDtypeStruct((batch, seq, LANES), F32),
        grid=(batch,),
        in_specs=[pl.BlockSpec((1, seq, d), lambda b: (b, 0, 0)),
                  pl.BlockSpec((d, LANES), lambda b: (0, 0)),
                  pl.BlockSpec((1, LANES), lambda b: (0, 0))],
        out_specs=pl.BlockSpec((1, seq, LANES), lambda b: (b, 0, 0)),
        compiler_params=_cparams("parallel"),
        name="fox_decay_cumsum",
    )(x3, wf, bfv)


def _split3_bf16(v):
    hi = v.astype(BF16).astype(F32)
    r = v - hi
    mid = r.astype(BF16).astype(F32)
    lo = (r - mid).astype(BF16).astype(F32)
    return hi, mid, lo


def _fox_kernel(q_ref, k_ref, v_ref, ccol_ref, o_ref, spec_ref, *, tq, nq, pps):
    hp = pl.program_id(1)
    qi = pl.program_id(2)
    n_heads = 2 * pps
    ones0 = 3 * n_heads
    lane = lax.broadcasted_iota(jnp.int32, (1, LANES), 1)
    lo_half = lane < HEAD_DIM
    heads = [(pr, e) for pr in range(pps) for e in range(2)]

    @pl.when(qi == 0)
    def _():
        parts = jnp.concatenate(_split3_bf16(-ccol_ref[0]), axis=1).astype(BF16)
        r = lax.broadcasted_iota(jnp.int32, (3 * LANES, LANES), 0)
        c = lax.broadcasted_iota(jnp.int32, (3 * LANES, LANES), 1)
        head_i = r % LANES - hp * n_heads
        pick = (head_i >= 0) & (head_i < n_heads) & (c == 3 * head_i + r // LANES)
        sel = jnp.where(pick, 1.0, 0.0).astype(BF16)
        spec = jnp.dot(parts, sel, preferred_element_type=F32)
        spec = jnp.where((lane >= ones0) & (lane < ones0 + 3), 1.0, spec)
        spec_ref[...] = spec.astype(BF16)

    ct_all = ccol_ref[0, pl.ds(pl.multiple_of(qi * tq, tq), tq), :]
    lane_c = lax.broadcasted_iota(jnp.int32, ct_all.shape, 1)
    q_heads = []
    for i, (pr, e) in enumerate(heads):
        qp = q_ref[0, :, pr * LANES:(pr + 1) * LANES]
        own = lo_half if e == 0 else jnp.logical_not(lo_half)
        qa = jnp.where(own, qp, jnp.zeros_like(qp))
        ct = jnp.sum(jnp.where(lane_c == hp * n_heads + i, ct_all, 0.0), axis=1, keepdims=True)
        hi, mid, lo = _split3_bf16(ct)
        qs = jnp.where((lane >= 3 * i) & (lane < 3 * i + 3), 1.0, 0.0)
        qs = jnp.where(lane == ones0, hi, qs)
        qs = jnp.where(lane == ones0 + 1, mid, qs)
        qs = jnp.where(lane == ones0 + 2, lo, qs)
        q_heads.append(jnp.concatenate([qa, qs.astype(BF16)], axis=1))
    rr = lax.broadcasted_iota(jnp.int32, (tq, tq), 0)
    cc_i = lax.broadcasted_iota(jnp.int32, (tq, tq), 1)
    causal = rr >= cc_i
    nchunk = tq // LANES

    def chunks(a):
        return [a[:, i * LANES:(i + 1) * LANES] for i in range(nchunk)]

    def attend(n):
        outs = []
        for idx, (pr, e) in enumerate(heads):
            m = None
            l_part = jnp.zeros((tq, LANES), F32)
            o = jnp.zeros((tq, LANES), F32)
            for j in range(n + 1):
                rows = slice(j * tq, (j + 1) * tq)
                kj = jnp.concatenate([k_ref[0, rows, pr * LANES:(pr + 1) * LANES],
                                      spec_ref[rows, :]], axis=1)
                s = lax.dot_general(q_heads[idx], kj, (((1,), (1,)), ((), ())),
                                    preferred_element_type=F32)
                if j == n:
                    s = jnp.where(causal, s, NEG)
                bm = jnp.max(functools.reduce(jnp.maximum, chunks(s)), axis=1, keepdims=True)
                if m is None:
                    m = bm
                else:
                    m_new = jnp.maximum(m, bm)
                    a = jnp.exp2(m - m_new)
                    l_part = l_part * a
                    o = o * a
                    m = m_new
                p = jnp.exp2(s - m)
                l_part = l_part + functools.reduce(lambda a, b: a + b, chunks(p))
                o = o + jnp.dot(p.astype(BF16), v_ref[0, rows, pr * LANES:(pr + 1) * LANES],
                                preferred_element_type=F32)
            outs.append(o / jnp.sum(l_part, axis=1, keepdims=True))
        for pr in range(pps):
            o_ref[0, :, pr * LANES:(pr + 1) * LANES] = jnp.where(
                lo_half, outs[2 * pr], outs[2 * pr + 1]).astype(o_ref.dtype)

    lax.switch(qi, [functools.partial(attend, n) for n in range(nq)])


FOX_PAIRS_PER_STEP = 4


def _fox_attention(proj, ccol, batch, seq):
    tq = min(256, seq)
    nq = seq // tq
    hd_all = FOX_HEADS * HEAD_DIM
    pps = FOX_PAIRS_PER_STEP
    width = pps * LANES
    groups = hd_all // width
    kern = functools.partial(_fox_kernel, tq=tq, nq=nq, pps=pps)
    return pl.pallas_call(
        kern,
        out_shape=jax.ShapeDtypeStruct((batch, seq, hd_all), BF16),
        grid=(batch, groups, nq),
        in_specs=[pl.BlockSpec((1, tq, width), lambda b, h, i: (b, i, h)),
                  pl.BlockSpec((1, seq, width), lambda b, h, i: (b, 0, groups + h)),
                  pl.BlockSpec((1, seq, width), lambda b, h, i: (b, 0, 2 * groups + h)),
                  pl.BlockSpec((1, seq, LANES), lambda b, h, i: (b, 0, 0))],
        out_specs=pl.BlockSpec((1, tq, width), lambda b, h, i: (b, i, h)),
        scratch_shapes=[pltpu.VMEM((seq, LANES), BF16)],
        compiler_params=_cparams("parallel", "parallel", "arbitrary"),
        name="fox_attention",
    )(proj, proj, proj, ccol)


def _layer_norm(y, gain, bias):
    mu = jnp.mean(y, axis=-1, keepdims=True)
    d = y - mu
    var = jnp.mean(d * d, axis=-1, keepdims=True)
    return d * lax.rsqrt(var + LN_EPS) * gain + bias


def _split_bf16(v):
    hi = v.astype(BF16)
    lo = (v - hi.astype(F32)).astype(BF16)
    return hi, lo


def _route(logits_t):
    rows = [logits_t[e:e + 1, :] for e in range(N_EXPERTS)]
    mx = functools.reduce(jnp.maximum, rows)
    ex = [jnp.exp(v - mx) for v in rows]
    den = functools.reduce(lambda a, b: a + b, ex)
    sc = [v / den for v in ex]
    gscore = []
    for g in range(N_GROUPS):
        mem = sc[g * EXPERTS_PER_GROUP:(g + 1) * EXPERTS_PER_GROUP]
        pairs = [mem[i] + mem[j] for i in range(4) for j in range(i + 1, 4)]
        gscore.append(functools.reduce(jnp.maximum, pairs))
    gmax = functools.reduce(jnp.maximum, gscore)
    taken = None
    gates = []
    for g in range(N_GROUPS):
        eq = gscore[g] == gmax
        best = eq if taken is None else eq & jnp.logical_not(taken)
        taken = eq if taken is None else taken | eq
        mem = sc[g * EXPERTS_PER_GROUP:(g + 1) * EXPERTS_PER_GROUP]
        sel = []
        for i in range(4):
            rank = jnp.zeros_like(mem[i])
            for j in range(4):
                if j == i:
                    continue
                ahead = (mem[j] > mem[i]) | ((mem[j] == mem[i]) & (j < i))
                rank = rank + jnp.where(ahead, 1.0, 0.0)
            sel.append(best & (rank < 2.0))
        tot = functools.reduce(lambda a, b: a + b,
                               [jnp.where(sel[i], mem[i], 0.0) for i in range(4)])
        for i in range(4):
            gates.append(jnp.where(sel[i], mem[i] / tot, 0.0))
    return gates


OUTPROJ_SUBTILES = 2


def _outproj_kernel(a_ref, wo_ref, x_ref, g_ref, b_ref, wr_ref, br_ref, x1_ref, gate_ref):
    sub = x_ref.shape[0] // OUTPROJ_SUBTILES
    wo = wo_ref[...].astype(BF16)
    for h in range(OUTPROJ_SUBTILES):
        rows = slice(h * sub, (h + 1) * sub)
        mix = jnp.dot(a_ref[rows, :], wo, preferred_element_type=F32)
        x1 = _layer_norm(ALPHA * x_ref[rows, :] + mix, g_ref[...], b_ref[...])
        x1_ref[rows, :] = x1
        hi, lo = _split_bf16(x1)
        from_hi = jnp.dot(hi, wr_ref[...], preferred_element_type=F32)
        from_lo = jnp.dot(lo, wr_ref[...], preferred_element_type=F32)
        logits = (from_hi[:, :LANES] + (from_lo[:, :LANES] + from_hi[:, LANES:])) + br_ref[...]
        routed = _route(logits.T[:N_EXPERTS, :])
        pad = jnp.zeros((LANES - len(routed), sub), F32)
        gate_ref[rows, :] = jnp.concatenate(list(routed) + [pad], axis=0).T


def _outproj_ln_router(a2, wo_stack, layer, x2, gain, bias, wr, br):
    t_tokens, d = x2.shape
    tm = min(1024, t_tokens)
    row = lambda i: (i, 0)
    fixed = lambda i: (0, 0)
    return pl.pallas_call(
        _outproj_kernel,
        out_shape=(jax.ShapeDtypeStruct((t_tokens, d), F32),
                   jax.ShapeDtypeStruct((t_tokens, LANES), F32)),
        grid=(t_tokens // tm,),
        in_specs=[pl.BlockSpec((tm, d), row),
                  pl.BlockSpec((None, d, d), lambda i: (layer, 0, 0)),
                  pl.BlockSpec((tm, d), row),
                  pl.BlockSpec((1, d), fixed),
                  pl.BlockSpec((1, d), fixed),
                  pl.BlockSpec((d, 2 * LANES), fixed),
                  pl.BlockSpec((1, LANES), fixed)],
        out_specs=(pl.BlockSpec((tm, d), row), pl.BlockSpec((tm, LANES), row)),
        compiler_params=_cparams("parallel"),
        name="outproj_ln_router",
    )(a2, wo_stack, x2, gain, bias, wr, br)


def _moe_kernel(x_ref, gate_ref, wg_ref, wu_ref, wd_ref, g_ref, b_ref, o_ref, xb_ref, acc_ref):
    grp = pl.program_id(1)

    @pl.when(grp == 0)
    def _():
        xb_ref[...] = x_ref[...].astype(BF16)
        acc_ref[...] = jnp.zeros_like(acc_ref)

    gates = gate_ref[...]
    lane = lax.broadcasted_iota(jnp.int32, gates.shape, 1)
    acts = []
    for k in range(EXPERTS_PER_GROUP):
        xb = xb_ref[...]
        hg = jnp.dot(xb, wg_ref[k].astype(BF16), preferred_element_type=F32)
        hu = jnp.dot(xb, wu_ref[k].astype(BF16), preferred_element_type=F32)
        ge = jnp.sum(jnp.where(lane == grp * EXPERTS_PER_GROUP + k, gates, 0.0),
                     axis=1, keepdims=True)
        acts.append((hg * jax.nn.sigmoid(hg) * hu * ge).astype(BF16))
    wd_all = wd_ref[...].astype(BF16).reshape(EXPERTS_PER_GROUP * D_EXPERT, wd_ref.shape[2])
    acc_ref[...] += jnp.dot(jnp.concatenate(acts, axis=1), wd_all, preferred_element_type=F32)

    @pl.when(grp == N_GROUPS - 1)
    def _():
        o_ref[...] = _layer_norm(ALPHA * x_ref[...] + acc_ref[...], g_ref[...], b_ref[...])


def _moe_ln(x2, gates, w_gate, w_up, w_down, layer, gain, bias):
    t_tokens, d = x2.shape
    tm = min(1024, t_tokens)
    row = lambda i, e: (i, 0)
    fixed = lambda i, e: (0, 0)
    per_group = lambda i, e: (layer, e, 0, 0)
    return pl.pallas_call(
        _moe_kernel,
        out_shape=jax.ShapeDtypeStruct((t_tokens, d), F32),
        grid=(t_tokens // tm, N_GROUPS),
        in_specs=[pl.BlockSpec((tm, d), row),
                  pl.BlockSpec((tm, LANES), row),
                  pl.BlockSpec((None, EXPERTS_PER_GROUP, d, D_EXPERT), per_group),
                  pl.BlockSpec((None, EXPERTS_PER_GROUP, d, D_EXPERT), per_group),
                  pl.BlockSpec((None, EXPERTS_PER_GROUP, D_EXPERT, d), per_group),
                  pl.BlockSpec((1, d), fixed),
                  pl.BlockSpec((1, d), fixed)],
        out_specs=pl.BlockSpec((tm, d), row),
        scratch_shapes=[pltpu.VMEM((tm, d), BF16), pltpu.VMEM((tm, d), F32)],
        compiler_params=_cparams("parallel", "arbitrary"),
        name="moe_ln",
    )(x2, gates, w_gate, w_up, w_down, gain, bias)


def _pad_lanes(w):
    return jnp.pad(w, ((0, 0), (0, LANES - w.shape[1])))


def kernel(x, ln_gain, ln_bias, swa_w_in, swa_sinks, swa_w_o, fox_w_in, fox_b_f, fox_w_o,
           w_router, b_router, w_gate, w_up, w_down):
    batch, seq, d = x.shape
    t_tokens = batch * seq
    tables = _rope_tables(seq)
    wr = jnp.concatenate(_split_bf16(_pad_lanes(w_router)), axis=1)
    br = _pad_lanes(b_router[None, :])
    fox_qkv = 3 * FOX_HEADS * HEAD_DIM
    x2 = x.reshape(t_tokens, d)
    for i in range(DEPTH):
        j = i // 2
        if i % 2 == 0:
            proj = _in_projection(x2, swa_w_in, j, swa_w_in.shape[2], seq,
                                  q_cols=SWA_Q_HEADS * HEAD_DIM, q_scale=QK_SCALE * LOG2E,
                                  rope_cols=(SWA_Q_HEADS + SWA_KV_HEADS) * HEAD_DIM,
                                  tables=tables)
            attn = _swa_attention(proj.reshape(batch, seq, -1), swa_sinks[j] * LOG2E, batch, seq)
            w_o = swa_w_o
        else:
            proj = _in_projection(x2, fox_w_in, j, fox_qkv, seq,
                                  q_cols=FOX_HEADS * HEAD_DIM, q_scale=QK_SCALE * LOG2E,
                                  rope_cols=0, tables=tables)
            ccol = _decay_cumsum(x2.reshape(batch, seq, d),
                                 _pad_lanes(fox_w_in[j, :, fox_qkv:]).astype(BF16),
                                 _pad_lanes(fox_b_f[j][None, :]))
            attn = _fox_attention(proj.reshape(batch, seq, -1), ccol, batch, seq)
            w_o = fox_w_o
        x1, gates = _outproj_ln_router(attn.reshape(t_tokens, d), w_o, j, x2,
                                       ln_gain[i, 0][None, :], ln_bias[i, 0][None, :],
                                       wr, br)
        x2 = _moe_ln(x1, gates, w_gate, w_up, w_down, i,
                     ln_gain[i, 1][None, :], ln_bias[i, 1][None, :])
    return x2.reshape(batch, seq, d)
```

```python
import functools
import math

import jax
import jax.numpy as jnp
from jax import lax
from jax.experimental import pallas as pl
from jax.experimental.pallas import tpu as pltpu

F32 = jnp.float32
BF16 = jnp.bfloat16

HEAD_DIM = 64
ROT_DIM = 16
ROT_HALF = ROT_DIM // 2
ROPE_THETA = 500000.0
SWA_Q_HEADS = 16
SWA_KV_HEADS = 4
SWA_BLOCK = 128
FOX_HEADS = 16
N_EXPERTS = 16
N_GROUPS = 4
EXPERTS_PER_GROUP = 4
D_EXPERT = 256
DEPTH = 4
ALPHA = (2.0 * DEPTH) ** 0.25
LN_EPS = 1e-5
NEG = -1e30
QK_SCALE = HEAD_DIM ** -0.5
LOG2E = math.log2(math.e)

LANES = 128
VMEM_LIMIT = 56 * 1024 * 1024


def _cparams(*sem):
    return pltpu.CompilerParams(dimension_semantics=sem, vmem_limit_bytes=VMEM_LIMIT)


def _inproj_kernel(x_ref, w_ref, cos_ref, sa_ref, sb_ref, o_ref, *, tn, q_cols, q_scale,
                   rope_cols):
    xb = x_ref[...].astype(BF16)
    n = o_ref.shape[1]
    for c0 in range(0, n, tn):
        acc = jnp.dot(xb, w_ref[:, c0:c0 + tn].astype(BF16), preferred_element_type=F32)
        n_rope = min(max(rope_cols - c0, 0), tn)
        if n_rope:
            reps = n_rope // LANES
            head = acc[:, :n_rope]
            head = (head * jnp.tile(cos_ref[...], (1, reps))
                    + pltpu.roll(head, ROT_HALF, 1) * jnp.tile(sa_ref[...], (1, reps))
                    + pltpu.roll(head, n_rope - ROT_HALF, 1) * jnp.tile(sb_ref[...], (1, reps)))
            acc = head if n_rope == tn else jnp.concatenate([head, acc[:, n_rope:]], axis=1)
        if c0 + tn <= q_cols:
            acc = acc * q_scale
        else:
            assert c0 >= q_cols, "q columns must end on a chunk boundary"
        o_ref[:, c0:c0 + tn] = acc.astype(o_ref.dtype)


def _rope_tables(seq):
    inv_freq = jnp.power(ROPE_THETA, -jnp.arange(ROT_HALF, dtype=F32) * (2.0 / ROT_DIM))
    ang = jnp.arange(seq, dtype=F32)[:, None] * inv_freq[None, :]
    cos, sin = jnp.cos(ang), jnp.sin(ang)
    ones = jnp.ones((seq, HEAD_DIM - ROT_DIM), F32)
    zeros = jnp.zeros((seq, HEAD_DIM - ROT_DIM), F32)
    z8 = jnp.zeros((seq, ROT_HALF), F32)
    cos_h = jnp.concatenate([cos, cos, ones], axis=1)
    sa_h = jnp.concatenate([z8, sin, zeros], axis=1)
    sb_h = jnp.concatenate([-sin, z8, zeros], axis=1)
    rep = LANES // HEAD_DIM
    return tuple(jnp.tile(t, (1, rep)) for t in (cos_h, sa_h, sb_h))


def _in_projection(x2, w_stack, layer, n, seq, *, q_cols, q_scale, rope_cols, tables):
    t_tokens, d = x2.shape
    tm = min(1024, seq)
    tn = 512
    assert t_tokens % tm == 0 and seq % tm == 0 and n % tn == 0
    pos_blocks = seq // tm
    kern = functools.partial(_inproj_kernel, tn=tn, q_cols=q_cols, q_scale=q_scale,
                             rope_cols=rope_cols)
    tab_spec = pl.BlockSpec((tm, LANES), lambda i: (i % pos_blocks, 0))
    return pl.pallas_call(
        kern,
        out_shape=jax.ShapeDtypeStruct((t_tokens, n), BF16),
        grid=(t_tokens // tm,),
        in_specs=[pl.BlockSpec((tm, d), lambda i: (i, 0)),
                  pl.BlockSpec((None,) + w_stack.shape[1:], lambda i: (layer, 0, 0),
                               pipeline_mode=pl.Buffered(1)),
                  tab_spec, tab_spec, tab_spec],
        out_specs=pl.BlockSpec((tm, n), lambda i: (i, 0)),
        compiler_params=_cparams("parallel"),
        name="in_projection",
    )(x2, w_stack, *tables)


SWA_BLOCKS_PER_STEP = 4


def _swa_kernel(sink_ref, q_ref, kp_ref, kc_ref, vp_ref, vc_ref, o_ref):
    n = pl.program_id(1)
    blk = SWA_BLOCK
    nsub = q_ref.shape[1] // blk
    lane = lax.broadcasted_iota(jnp.int32, (1, LANES), 1)
    lo = lane < HEAD_DIM
    r = lax.broadcasted_iota(jnp.int32, (blk, 2 * blk), 0)
    c = lax.broadcasted_iota(jnp.int32, (blk, 2 * blk), 1)
    band = (c > r) & (c <= r + blk)
    allowed = [band & ((c >= blk) | (n > 0))] + [band] * (nsub - 1)
    k_all = jnp.concatenate([kp_ref[0], kc_ref[0]], axis=0).astype(F32)
    v_all = jnp.concatenate([vp_ref[0], vc_ref[0]], axis=0).astype(F32)
    for g in range(SWA_KV_HEADS):
        grp = g // 2
        kf = k_all[:, grp * LANES:(grp + 1) * LANES]
        vf = v_all[:, grp * LANES:(grp + 1) * LANES]
        kr = pltpu.roll(kf, HEAD_DIM, 1)
        vr = pltpu.roll(vf, HEAD_DIM, 1)
        own_lo = (g % 2 == 0)
        k_src_lo, k_src_hi = (kf, kr) if own_lo else (kr, kf)
        k_lo = jnp.where(lo, k_src_lo, 0.0).astype(BF16)
        k_hi = jnp.where(lo, 0.0, k_src_hi).astype(BF16)
        v_lo = jnp.where(lo, vf if own_lo else vr, 1.0).astype(BF16)
        v_hi = jnp.where(lo, 1.0, vr if own_lo else vf).astype(BF16)
        for sub in range(nsub):
            qrows = slice(sub * blk, (sub + 1) * blk)
            krows = slice(sub * blk, (sub + 2) * blk)
            for pp in range(2):
                p = 2 * g + pp
                qp = q_ref[0, qrows, p * LANES:(p + 1) * LANES]
                outs, sink_terms = [], []
                for e, (kk, vv) in enumerate(((k_lo, v_lo), (k_hi, v_hi))):
                    s = lax.dot_general(qp, kk[krows], (((1,), (1,)), ((), ())),
                                        preferred_element_type=F32)
                    s = jnp.where(allowed[sub], s, NEG)
                    sink = sink_ref[2 * p + e]
                    m = jnp.maximum(jnp.max(s, axis=1, keepdims=True), sink)
                    pr = jnp.exp2(s - m)
                    outs.append(jnp.dot(pr.astype(BF16), vv[krows], preferred_element_type=F32))
                    sink_terms.append(jnp.exp2(sink - m))
                num = jnp.where(lo, outs[0], outs[1])
                den = (pltpu.roll(jnp.where(lo, outs[1], outs[0]), HEAD_DIM, 1)
                       + jnp.where(lo, sink_terms[0], sink_terms[1]))
                o_ref[0, qrows, p * LANES:(p + 1) * LANES] = (num / den).astype(o_ref.dtype)


def _swa_attention(proj, sinks, batch, seq):
    blk = SWA_BLOCK
    nb = seq // blk
    qd = SWA_Q_HEADS * HEAD_DIM
    kd = SWA_KV_HEADS * HEAD_DIM
    kcol, vcol = qd // kd, qd // kd + 1
    nsub = min(SWA_BLOCKS_PER_STEP, nb)
    run = nsub * blk
    assert nb % nsub == 0
    prev = lambda b, n: jnp.maximum(n * nsub - 1, 0)
    return pl.pallas_call(
        _swa_kernel,
        out_shape=jax.ShapeDtypeStruct((batch, seq, qd), BF16),
        grid=(batch, nb // nsub),
        in_specs=[pl.BlockSpec(memory_space=pltpu.SMEM),
                  pl.BlockSpec((1, run, qd), lambda b, n: (b, n, 0)),
                  pl.BlockSpec((1, blk, kd), lambda b, n: (b, prev(b, n), kcol)),
                  pl.BlockSpec((1, run, kd), lambda b, n: (b, n, kcol)),
                  pl.BlockSpec((1, blk, kd), lambda b, n: (b, prev(b, n), vcol)),
                  pl.BlockSpec((1, run, kd), lambda b, n: (b, n, vcol))],
        out_specs=pl.BlockSpec((1, run, qd), lambda b, n: (b, n, 0)),
        compiler_params=_cparams("parallel", "arbitrary"),
        name="swa_attention",
    )(sinks, proj, proj, proj, proj, proj)


def _decay_kernel(x_ref, wf_ref, bf_ref, ccol_ref):
    seq = x_ref.shape[1]
    f = jnp.dot(x_ref[0].astype(BF16), wf_ref[...], preferred_element_type=F32) + bf_ref[...]
    ls = jnp.minimum(f, 0.0) - jnp.log1p(jnp.exp(-jnp.abs(f)))
    row = lax.broadcasted_iota(jnp.int32, ls.shape, 0)
    c = ls
    k = 1
    while k < seq:
        c = c + jnp.where(row >= k, pltpu.roll(c, k, 0), 0.0)
        k *= 2
    ccol_ref[0] = c * LOG2E


def _decay_cumsum(x3, wf, bfv):
    batch, seq, d = x3.shape
    return pl.pallas_call(
        _decay_kernel,
        out_shape=jax.ShapeDtypeStruct((batch, seq, LANES), F32),
        grid=(batch,),
        in_specs=[pl.BlockSpec((1, seq, d), lambda b: (b, 0, 0)),
                  pl.BlockSpec((d, LANES), lambda b: (0, 0)),
                  pl.BlockSpec((1, LANES), lambda b: (0, 0))],
        out_specs=pl.BlockSpec((1, seq, LANES), lambda b: (b, 0, 0)),
        compiler_params=_cparams("parallel"),
        name="fox_decay_cumsum",
    )(x3, wf, bfv)


def _split3_bf16(v):
    hi = v.astype(BF16).astype(F32)
    r = v - hi
    mid = r.astype(BF16).astype(F32)
    lo = (r - mid).astype(BF16).astype(F32)
    return hi, mid, lo


def _fox_kernel(q_ref, k_ref, v_ref, ccol_ref, o_ref, spec_ref, *, tq, nq, pps):
    hp = pl.program_id(1)
    qi = pl.program_id(2)
    n_heads = 2 * pps
    ones0 = 3 * n_heads
    lane = lax.broadcasted_iota(jnp.int32, (1, LANES), 1)
    lo_half = lane < HEAD_DIM
    heads = [(pr, e) for pr in range(pps) for e in range(2)]

    @pl.when(qi == 0)
    def _():
        parts = jnp.concatenate(_split3_bf16(-ccol_ref[0]), axis=1).astype(BF16)
        r = lax.broadcasted_iota(jnp.int32, (3 * LANES, LANES), 0)
        c = lax.broadcasted_iota(jnp.int32, (3 * LANES, LANES), 1)
        head_i = r % LANES - hp * n_heads
        pick = (head_i >= 0) & (head_i < n_heads) & (c == 3 * head_i + r // LANES)
        sel = jnp.where(pick, 1.0, 0.0).astype(BF16)
        spec = jnp.dot(parts, sel, preferred_element_type=F32)
        spec = jnp.where((lane >= ones0) & (lane < ones0 + 3), 1.0, spec)
        spec_ref[...] = spec.astype(BF16)

    qsub = q_ref.shape[1] // tq
    ct_all = ccol_ref[0, pl.ds(pl.multiple_of(qi * qsub * tq, tq), qsub * tq), :]
    lane_c = lax.broadcasted_iota(jnp.int32, (tq, LANES), 1)
    q_heads = []
    for sub in range(qsub):
        qrows = slice(sub * tq, (sub + 1) * tq)
        for i, (pr, e) in enumerate(heads):
            qp = q_ref[0, qrows, pr * LANES:(pr + 1) * LANES]
            own = lo_half if e == 0 else jnp.logical_not(lo_half)
            qa = jnp.where(own, qp, jnp.zeros_like(qp))
            ct = jnp.sum(jnp.where(lane_c == hp * n_heads + i, ct_all[qrows], 0.0),
                         axis=1, keepdims=True)
            hi, mid, lo = _split3_bf16(ct)
            qs = jnp.where((lane >= 3 * i) & (lane < 3 * i + 3), 1.0, 0.0)
            qs = jnp.where(lane == ones0, hi, qs)
            qs = jnp.where(lane == ones0 + 1, mid, qs)
            qs = jnp.where(lane == ones0 + 2, lo, qs)
            q_heads.append(jnp.concatenate([qa, qs.astype(BF16)], axis=1))
    rr = lax.broadcasted_iota(jnp.int32, (tq, tq), 0)
    cc_i = lax.broadcasted_iota(jnp.int32, (tq, tq), 1)
    causal = rr >= cc_i
    nchunk = tq // LANES

    def chunks(a):
        return [a[:, i * LANES:(i + 1) * LANES] for i in range(nchunk)]

    def attend(step):
        for sub in range(qsub):
            attend_tile(step * qsub + sub, sub)

    def attend_tile(n, sub):
        outs = []
        for idx, (pr, e) in enumerate(heads):
            m = None
            l_part = jnp.zeros((tq, LANES), F32)
            o = jnp.zeros((tq, LANES), F32)
            for j in range(n + 1):
                rows = slice(j * tq, (j + 1) * tq)
                kj = jnp.concatenate([k_ref[0, rows, pr * LANES:(pr + 1) * LANES],
                                      spec_ref[rows, :]], axis=1)
                s = lax.dot_general(q_heads[sub * n_heads + idx], kj, (((1,), (1,)), ((), ())),
                                    preferred_element_type=F32)
                if j == n:
                    s = jnp.where(causal, s, NEG)
                bm = jnp.max(functools.reduce(jnp.maximum, chunks(s)), axis=1, keepdims=True)
                if m is None:
                    m = bm
                else:
                    m_new = jnp.maximum(m, bm)
                    a = jnp.exp2(m - m_new)
                    l_part = l_part * a
                    o = o * a
                    m = m_new
                p = jnp.exp2(s - m)
                l_part = l_part + functools.reduce(lambda a, b: a + b, chunks(p))
                o = o + jnp.dot(p.astype(BF16), v_ref[0, rows, pr * LANES:(pr + 1) * LANES],
                                preferred_element_type=F32)
            outs.append(o / jnp.sum(l_part, axis=1, keepdims=True))
        for pr in range(pps):
            o_ref[0, sub * tq:(sub + 1) * tq, pr * LANES:(pr + 1) * LANES] = jnp.where(
                lo_half, outs[2 * pr], outs[2 * pr + 1]).astype(o_ref.dtype)

    lax.switch(qi, [functools.partial(attend, step) for step in range(nq // qsub)])


FOX_PAIRS_PER_STEP = 4
FOX_TILES_PER_STEP = 2


def _fox_attention(proj, ccol, batch, seq):
    tq = min(256, seq)
    nq = seq // tq
    hd_all = FOX_HEADS * HEAD_DIM
    pps = FOX_PAIRS_PER_STEP
    width = pps * LANES
    groups = hd_all // width
    qsub = min(FOX_TILES_PER_STEP, nq)
    assert nq % qsub == 0
    kern = functools.partial(_fox_kernel, tq=tq, nq=nq, pps=pps)
    return pl.pallas_call(
        kern,
        out_shape=jax.ShapeDtypeStruct((batch, seq, hd_all), BF16),
        grid=(batch, groups, nq // qsub),
        in_specs=[pl.BlockSpec((1, qsub * tq, width), lambda b, h, i: (b, i, h)),
                  pl.BlockSpec((1, seq, width), lambda b, h, i: (b, 0, groups + h)),
                  pl.BlockSpec((1, seq, width), lambda b, h, i: (b, 0, 2 * groups + h)),
                  pl.BlockSpec((1, seq, LANES), lambda b, h, i: (b, 0, 0))],
        out_specs=pl.BlockSpec((1, qsub * tq, width), lambda b, h, i: (b, i, h)),
        scratch_shapes=[pltpu.VMEM((seq, LANES), BF16)],
        compiler_params=_cparams("parallel", "parallel", "arbitrary"),
        name="fox_attention",
    )(proj, proj, proj, ccol)


def _layer_norm(y, gain, bias):
    mu = jnp.mean(y, axis=-1, keepdims=True)
    d = y - mu
    var = jnp.mean(d * d, axis=-1, keepdims=True)
    return d * lax.rsqrt(var + LN_EPS) * gain + bias


def _split_bf16(v):
    hi = v.astype(BF16)
    lo = (v - hi.astype(F32)).astype(BF16)
    return hi, lo


def _route(logits_t):
    rows = [logits_t[e:e + 1, :] for e in range(N_EXPERTS)]
    mx = functools.reduce(jnp.maximum, rows)
    ex = [jnp.exp(v - mx) for v in rows]
    den = functools.reduce(lambda a, b: a + b, ex)
    sc = [v / den for v in ex]
    gscore = []
    for g in range(N_GROUPS):
        mem = sc[g * EXPERTS_PER_GROUP:(g + 1) * EXPERTS_PER_GROUP]
        pairs = [mem[i] + mem[j] for i in range(4) for j in range(i + 1, 4)]
        gscore.append(functools.reduce(jnp.maximum, pairs))
    gmax = functools.reduce(jnp.maximum, gscore)
    taken = None
    gates = []
    for g in range(N_GROUPS):
        eq = gscore[g] == gmax
        best = eq if taken is None else eq & jnp.logical_not(taken)
        taken = eq if taken is None else taken | eq
        mem = sc[g * EXPERTS_PER_GROUP:(g + 1) * EXPERTS_PER_GROUP]
        sel = []
        for i in range(4):
            rank = jnp.zeros_like(mem[i])
            for j in range(4):
                if j == i:
                    continue
                ahead = (mem[j] > mem[i]) | ((mem[j] == mem[i]) & (j < i))
                rank = rank + jnp.where(ahead, 1.0, 0.0)
            sel.append(best & (rank < 2.0))
        tot = functools.reduce(lambda a, b: a + b,
                               [jnp.where(sel[i], mem[i], 0.0) for i in range(4)])
        for i in range(4):
            gates.append(jnp.where(sel[i], mem[i] / tot, 0.0))
    return gates


OUTPROJ_SUBTILES = 2


def _outproj_kernel(a_ref, wo_ref, x_ref, g_ref, b_ref, wr_ref, br_ref, x1_ref, gate_ref):
    sub = x_ref.shape[0] // OUTPROJ_SUBTILES
    wo = wo_ref[...].astype(BF16)
    for h in range(OUTPROJ_SUBTILES):
        rows = slice(h * sub, (h + 1) * sub)
        mix = jnp.dot(a_ref[rows, :], wo, preferred_element_type=F32)
        x1 = _layer_norm(ALPHA * x_ref[rows, :] + mix, g_ref[...], b_ref[...])
        x1_ref[rows, :] = x1
        hi, lo = _split_bf16(x1)
        from_hi = jnp.dot(hi, wr_ref[...], preferred_element_type=F32)
        from_lo = jnp.dot(lo, wr_ref[...], preferred_element_type=F32)
        logits = (from_hi[:, :LANES] + (from_lo[:, :LANES] + from_hi[:, LANES:])) + br_ref[...]
        routed = _route(logits.T[:N_EXPERTS, :])
        pad = jnp.zeros((LANES - len(routed), sub), F32)
        gate_ref[rows, :] = jnp.concatenate(list(routed) + [pad], axis=0).T


def _outproj_ln_router(a2, wo_stack, layer, x2, gain, bias, wr, br):
    t_tokens, d = x2.shape
    tm = min(1024, t_tokens)
    row = lambda i: (i, 0)
    fixed = lambda i: (0, 0)
    return pl.pallas_call(
        _outproj_kernel,
        out_shape=(jax.ShapeDtypeStruct((t_tokens, d), F32),
                   jax.ShapeDtypeStruct((t_tokens, LANES), F32)),
        grid=(t_tokens // tm,),
        in_specs=[pl.BlockSpec((tm, d), row),
                  pl.BlockSpec((None, d, d), lambda i: (layer, 0, 0)),
                  pl.BlockSpec((tm, d), row),
                  pl.BlockSpec((1, d), fixed),
                  pl.BlockSpec((1, d), fixed),
                  pl.BlockSpec((d, 2 * LANES), fixed),
                  pl.BlockSpec((1, LANES), fixed)],
        out_specs=(pl.BlockSpec((tm, d), row), pl.BlockSpec((tm, LANES), row)),
        compiler_params=_cparams("parallel"),
        name="outproj_ln_router",
    )(a2, wo_stack, x2, gain, bias, wr, br)


def _moe_kernel(x_ref, gate_ref, wg_ref, wu_ref, wd_ref, g_ref, b_ref, o_ref, xb_ref, acc_ref):
    grp = pl.program_id(1)

    @pl.when(grp == 0)
    def _():
        xb_ref[...] = x_ref[...].astype(BF16)
        acc_ref[...] = jnp.zeros_like(acc_ref)

    gates = gate_ref[...]
    lane = lax.broadcasted_iota(jnp.int32, gates.shape, 1)
    acts = []
    for k in range(EXPERTS_PER_GROUP):
        xb = xb_ref[...]
        hg = jnp.dot(xb, wg_ref[k].astype(BF16), preferred_element_type=F32)
        hu = jnp.dot(xb, wu_ref[k].astype(BF16), preferred_element_type=F32)
        ge = jnp.sum(jnp.where(lane == grp * EXPERTS_PER_GROUP + k, gates, 0.0),
                     axis=1, keepdims=True)
        acts.append((hg * jax.nn.sigmoid(hg) * hu * ge).astype(BF16))
    wd_all = wd_ref[...].astype(BF16).reshape(EXPERTS_PER_GROUP * D_EXPERT, wd_ref.shape[2])
    acc_ref[...] += jnp.dot(jnp.concatenate(acts, axis=1), wd_all, preferred_element_type=F32)

    @pl.when(grp == N_GROUPS - 1)
    def _():
        o_ref[...] = _layer_norm(ALPHA * x_ref[...] + acc_ref[...], g_ref[...], b_ref[...])


def _moe_ln(x2, gates, w_gate, w_up, w_down, layer, gain, bias):
    t_tokens, d = x2.shape
    tm = min(1024, t_tokens)
    row = lambda i, e: (i, 0)
    fixed = lambda i, e: (0, 0)
    per_group = lambda i, e: (layer, e, 0, 0)
    return pl.pallas_call(
        _moe_kernel,
        out_shape=jax.ShapeDtypeStruct((t_tokens, d), F32),
        grid=(t_tokens // tm, N_GROUPS),
        in_specs=[pl.BlockSpec((tm, d), row),
                  pl.BlockSpec((tm, LANES), row),
                  pl.BlockSpec((None, EXPERTS_PER_GROUP, d, D_EXPERT), per_group),
                  pl.BlockSpec((None, EXPERTS_PER_GROUP, d, D_EXPERT), per_group),
                  pl.BlockSpec((None, EXPERTS_PER_GROUP, D_EXPERT, d), per_group),
                  pl.BlockSpec((1, d), fixed),
                  pl.BlockSpec((1, d), fixed)],
        out_specs=pl.BlockSpec((tm, d), row),
        scratch_shapes=[pltpu.VMEM((tm, d), BF16), pltpu.VMEM((tm, d), F32)],
        compiler_params=_cparams("parallel", "arbitrary"),
        name="moe_ln",
    )(x2, gates, w_gate, w_up, w_down, gain, bias)


def _pad_lanes(w):
    return jnp.pad(w, ((0, 0), (0, LANES - w.shape[1])))


def kernel(x, ln_gain, ln_bias, swa_w_in, swa_sinks, swa_w_o, fox_w_in, fox_b_f, fox_w_o,
           w_router, b_router, w_gate, w_up, w_down):
    batch, seq, d = x.shape
    t_tokens = batch * seq
    tables = _rope_tables(seq)
    wr = jnp.concatenate(_split_bf16(_pad_lanes(w_router)), axis=1)
    br = _pad_lanes(b_router[None, :])
    fox_qkv = 3 * FOX_HEADS * HEAD_DIM
    x2 = x.reshape(t_tokens, d)
    for i in range(DEPTH):
        j = i // 2
        if i % 2 == 0:
            proj = _in_projection(x2, swa_w_in, j, swa_w_in.shape[2], seq,
                                  q_cols=SWA_Q_HEADS * HEAD_DIM, q_scale=QK_SCALE * LOG2E,
                                  rope_cols=(SWA_Q_HEADS + SWA_KV_HEADS) * HEAD_DIM,
                                  tables=tables)
            attn = _swa_attention(proj.reshape(batch, seq, -1), swa_sinks[j] * LOG2E, batch, seq)
            w_o = swa_w_o
        else:
            proj = _in_projection(x2, fox_w_in, j, fox_qkv, seq,
                                  q_cols=FOX_HEADS * HEAD_DIM, q_scale=QK_SCALE * LOG2E,
                                  rope_cols=0, tables=tables)
            ccol = _decay_cumsum(x2.reshape(batch, seq, d),
                                 _pad_lanes(fox_w_in[j, :, fox_qkv:]).astype(BF16),
                                 _pad_lanes(fox_b_f[j][None, :]))
            attn = _fox_attention(proj.reshape(batch, seq, -1), ccol, batch, seq)
            w_o = fox_w_o
        x1, gates = _outproj_ln_router(attn.reshape(t_tokens, d), w_o, j, x2,
                                       ln_gain[i, 0][None, :], ln_bias[i, 0][None, :],
                                       wr, br)
        x2 = _moe_ln(x1, gates, w_gate, w_up, w_down, i,
                     ln_gain[i, 1][None, :], ln_bias[i, 1][None, :])
    return x2.reshape(batch, seq, d)
```

```python
import functools
import math

import jax
import jax.numpy as jnp
from jax import lax
from jax.experimental import pallas as pl
from jax.experimental.pallas import tpu as pltpu

F32 = jnp.float32
BF16 = jnp.bfloat16

HEAD_DIM = 64
ROT_DIM = 16
ROT_HALF = ROT_DIM // 2
ROPE_THETA = 500000.0
SWA_Q_HEADS = 16
SWA_KV_HEADS = 4
SWA_BLOCK = 128
FOX_HEADS = 16
N_EXPERTS = 16
N_GROUPS = 4
EXPERTS_PER_GROUP = 4
D_EXPERT = 256
DEPTH = 4
ALPHA = (2.0 * DEPTH) ** 0.25
LN_EPS = 1e-5
NEG = -1e30
QK_SCALE = HEAD_DIM ** -0.5
LOG2E = math.log2(math.e)

LANES = 128
VMEM_LIMIT = 56 * 1024 * 1024


def _cparams(*sem):
    return pltpu.CompilerParams(dimension_semantics=sem, vmem_limit_bytes=VMEM_LIMIT)


def _inproj_kernel(x_ref, w_ref, cos_ref, sa_ref, sb_ref, *rest, tn, q_cols, q_scale,
                   rope_cols, pos_blocks, with_decay):
    if with_decay:
        bf_ref, o_ref, ccol_ref, carry_ref = rest
    else:
        (o_ref,) = rest
    xb = x_ref[...].astype(BF16)
    n = o_ref.shape[1]
    for c0 in range(0, n, tn):
        acc = jnp.dot(xb, w_ref[:, c0:c0 + tn].astype(BF16), preferred_element_type=F32)
        n_rope = min(max(rope_cols - c0, 0), tn)
        if n_rope:
            reps = n_rope // LANES
            head = acc[:, :n_rope]
            head = (head * jnp.tile(cos_ref[...], (1, reps))
                    + pltpu.roll(head, ROT_HALF, 1) * jnp.tile(sa_ref[...], (1, reps))
                    + pltpu.roll(head, n_rope - ROT_HALF, 1) * jnp.tile(sb_ref[...], (1, reps)))
            acc = head if n_rope == tn else jnp.concatenate([head, acc[:, n_rope:]], axis=1)
        if c0 + tn <= q_cols:
            acc = acc * q_scale
        else:
            assert c0 >= q_cols, "q columns must end on a chunk boundary"
        o_ref[:, c0:c0 + tn] = acc.astype(o_ref.dtype)
    if with_decay:
        i = pl.program_id(0)
        tm, d = x_ref.shape
        n_f = w_ref.shape[1] - n
        wf = jnp.concatenate([w_ref[:, n:n + n_f].astype(BF16),
                              jnp.zeros((d, LANES - n_f), BF16)], axis=1)
        f = jnp.dot(xb, wf, preferred_element_type=F32) + bf_ref[...]
        ls = jnp.minimum(f, 0.0) - jnp.log1p(jnp.exp(-jnp.abs(f)))
        row = lax.broadcasted_iota(jnp.int32, ls.shape, 0)
        c = ls
        k = 1
        while k < tm:
            c = c + jnp.where(row >= k, pltpu.roll(c, k, 0), 0.0)
            k *= 2

        @pl.when(i % pos_blocks == 0)
        def _():
            carry_ref[...] = jnp.zeros_like(carry_ref)

        c = c + carry_ref[...]
        carry_ref[...] = c[tm - 1:tm, :]
        ccol_ref[...] = c * LOG2E


def _rope_tables(seq):
    inv_freq = jnp.power(ROPE_THETA, -jnp.arange(ROT_HALF, dtype=F32) * (2.0 / ROT_DIM))
    ang = jnp.arange(seq, dtype=F32)[:, None] * inv_freq[None, :]
    cos, sin = jnp.cos(ang), jnp.sin(ang)
    ones = jnp.ones((seq, HEAD_DIM - ROT_DIM), F32)
    zeros = jnp.zeros((seq, HEAD_DIM - ROT_DIM), F32)
    z8 = jnp.zeros((seq, ROT_HALF), F32)
    cos_h = jnp.concatenate([cos, cos, ones], axis=1)
    sa_h = jnp.concatenate([z8, sin, zeros], axis=1)
    sb_h = jnp.concatenate([-sin, z8, zeros], axis=1)
    rep = LANES // HEAD_DIM
    return tuple(jnp.tile(t, (1, rep)) for t in (cos_h, sa_h, sb_h))


def _in_projection(x2, w_stack, layer, n, seq, *, q_cols, q_scale, rope_cols, tables,
                   decay_bias=None):
    t_tokens, d = x2.shape
    tm = min(1024, seq)
    tn = 512
    assert t_tokens % tm == 0 and seq % tm == 0 and n % tn == 0
    pos_blocks = seq // tm
    with_decay = decay_bias is not None
    kern = functools.partial(_inproj_kernel, tn=tn, q_cols=q_cols, q_scale=q_scale,
                             rope_cols=rope_cols, pos_blocks=pos_blocks, with_decay=with_decay)
    tab_spec = pl.BlockSpec((tm, LANES), lambda i: (i % pos_blocks, 0))
    in_specs = [pl.BlockSpec((tm, d), lambda i: (i, 0)),
                pl.BlockSpec((None,) + w_stack.shape[1:], lambda i: (layer, 0, 0),
                             pipeline_mode=pl.Buffered(1)),
                tab_spec, tab_spec, tab_spec]
    proj_shape = jax.ShapeDtypeStruct((t_tokens, n), BF16)
    proj_spec = pl.BlockSpec((tm, n), lambda i: (i, 0))
    if not with_decay:
        return pl.pallas_call(
            kern, out_shape=proj_shape, grid=(t_tokens // tm,), in_specs=in_specs,
            out_specs=proj_spec, compiler_params=_cparams("parallel"), name="in_projection",
        )(x2, w_stack, *tables)
    return pl.pallas_call(
        kern,
        out_shape=(proj_shape, jax.ShapeDtypeStruct((t_tokens, LANES), F32)),
        grid=(t_tokens // tm,),
        in_specs=in_specs + [pl.BlockSpec((1, LANES), lambda i: (0, 0))],
        out_specs=(proj_spec, pl.BlockSpec((tm, LANES), lambda i: (i, 0))),
        scratch_shapes=[pltpu.VMEM((1, LANES), F32)],
        compiler_params=_cparams("arbitrary"),
        name="in_projection_decay",
    )(x2, w_stack, *tables, decay_bias)


SWA_BLOCKS_PER_STEP = 4


def _swa_kernel(sink_ref, q_ref, kp_ref, kc_ref, vp_ref, vc_ref, o_ref):
    n = pl.program_id(1)
    blk = SWA_BLOCK
    nsub = q_ref.shape[1] // blk
    lane = lax.broadcasted_iota(jnp.int32, (1, LANES), 1)
    lo = lane < HEAD_DIM
    r = lax.broadcasted_iota(jnp.int32, (blk, 2 * blk), 0)
    c = lax.broadcasted_iota(jnp.int32, (blk, 2 * blk), 1)
    band = (c > r) & (c <= r + blk)
    allowed = [band & ((c >= blk) | (n > 0))] + [band] * (nsub - 1)
    k_all = jnp.concatenate([kp_ref[0], kc_ref[0]], axis=0).astype(F32)
    v_all = jnp.concatenate([vp_ref[0], vc_ref[0]], axis=0).astype(F32)
    for g in range(SWA_KV_HEADS):
        grp = g // 2
        kf = k_all[:, grp * LANES:(grp + 1) * LANES]
        vf = v_all[:, grp * LANES:(grp + 1) * LANES]
        kr = pltpu.roll(kf, HEAD_DIM, 1)
        vr = pltpu.roll(vf, HEAD_DIM, 1)
        own_lo = (g % 2 == 0)
        k_src_lo, k_src_hi = (kf, kr) if own_lo else (kr, kf)
        k_lo = jnp.where(lo, k_src_lo, 0.0).astype(BF16)
        k_hi = jnp.where(lo, 0.0, k_src_hi).astype(BF16)
        v_lo = jnp.where(lo, vf if own_lo else vr, 1.0).astype(BF16)
        v_hi = jnp.where(lo, 1.0, vr if own_lo else vf).astype(BF16)
        for sub in range(nsub):
            qrows = slice(sub * blk, (sub + 1) * blk)
            krows = slice(sub * blk, (sub + 2) * blk)
            for pp in range(2):
                p = 2 * g + pp
                qp = q_ref[0, qrows, p * LANES:(p + 1) * LANES]
                outs, sink_terms = [], []
                for e, (kk, vv) in enumerate(((k_lo, v_lo), (k_hi, v_hi))):
                    s = lax.dot_general(qp, kk[krows], (((1,), (1,)), ((), ())),
                                        preferred_element_type=F32)
                    s = jnp.where(allowed[sub], s, NEG)
                    sink = sink_ref[2 * p + e]
                    m = jnp.maximum(jnp.max(s, axis=1, keepdims=True), sink)
                    pr = jnp.exp2(s - m)
                    outs.append(jnp.dot(pr.astype(BF16), vv[krows], preferred_element_type=F32))
                    sink_terms.append(jnp.exp2(sink - m))
                num = jnp.where(lo, outs[0], outs[1])
                den = (pltpu.roll(jnp.where(lo, outs[1], outs[0]), HEAD_DIM, 1)
                       + jnp.where(lo, sink_terms[0], sink_terms[1]))
                o_ref[0, qrows, p * LANES:(p + 1) * LANES] = (num / den).astype(o_ref.dtype)


def _swa_attention(proj, sinks, batch, seq):
    blk = SWA_BLOCK
    nb = seq // blk
    qd = SWA_Q_HEADS * HEAD_DIM
    kd = SWA_KV_HEADS * HEAD_DIM
    kcol, vcol = qd // kd, qd // kd + 1
    nsub = min(SWA_BLOCKS_PER_STEP, nb)
    run = nsub * blk
    assert nb % nsub == 0
    prev = lambda b, n: jnp.maximum(n * nsub - 1, 0)
    return pl.pallas_call(
        _swa_kernel,
        out_shape=jax.ShapeDtypeStruct((batch, seq, qd), BF16),
        grid=(batch, nb // nsub),
        in_specs=[pl.BlockSpec(memory_space=pltpu.SMEM),
                  pl.BlockSpec((1, run, qd), lambda b, n: (b, n, 0)),
                  pl.BlockSpec((1, blk, kd), lambda b, n: (b, prev(b, n), kcol)),
                  pl.BlockSpec((1, run, kd), lambda b, n: (b, n, kcol)),
                  pl.BlockSpec((1, blk, kd), lambda b, n: (b, prev(b, n), vcol)),
                  pl.BlockSpec((1, run, kd), lambda b, n: (b, n, vcol))],
        out_specs=pl.BlockSpec((1, run, qd), lambda b, n: (b, n, 0)),
        compiler_params=_cparams("parallel", "arbitrary"),
        name="swa_attention",
    )(sinks, proj, proj, proj, proj, proj)


def _split3_bf16(v):
    hi = v.astype(BF16).astype(F32)
    r = v - hi
    mid = r.astype(BF16).astype(F32)
    lo = (r - mid).astype(BF16).astype(F32)
    return hi, mid, lo


def _fox_kernel(q_ref, k_ref, v_ref, ccol_ref, o_ref, spec_ref, *, tq, nq, pps):
    hp = pl.program_id(1)
    qi = pl.program_id(2)
    n_heads = 2 * pps
    ones0 = 3 * n_heads
    lane = lax.broadcasted_iota(jnp.int32, (1, LANES), 1)
    lo_half = lane < HEAD_DIM
    heads = [(pr, e) for pr in range(pps) for e in range(2)]

    @pl.when(qi == 0)
    def _():
        parts = jnp.concatenate(_split3_bf16(-ccol_ref[0]), axis=1).astype(BF16)
        r = lax.broadcasted_iota(jnp.int32, (3 * LANES, LANES), 0)
        c = lax.broadcasted_iota(jnp.int32, (3 * LANES, LANES), 1)
        head_i = r % LANES - hp * n_heads
        pick = (head_i >= 0) & (head_i < n_heads) & (c == 3 * head_i + r // LANES)
        sel = jnp.where(pick, 1.0, 0.0).astype(BF16)
        spec = jnp.dot(parts, sel, preferred_element_type=F32)
        spec = jnp.where((lane >= ones0) & (lane < ones0 + 3), 1.0, spec)
        spec_ref[...] = spec.astype(BF16)

    qsub = q_ref.shape[1] // tq
    ct_all = ccol_ref[0, pl.ds(pl.multiple_of(qi * qsub * tq, tq), qsub * tq), :]
    lane_c = lax.broadcasted_iota(jnp.int32, (tq, LANES), 1)
    q_heads = []
    for sub in range(qsub):
        qrows = slice(sub * tq, (sub + 1) * tq)
        for i, (pr, e) in enumerate(heads):
            qp = q_ref[0, qrows, pr * LANES:(pr + 1) * LANES]
            own = lo_half if e == 0 else jnp.logical_not(lo_half)
            qa = jnp.where(own, qp, jnp.zeros_like(qp))
            ct = jnp.sum(jnp.where(lane_c == hp * n_heads + i, ct_all[qrows], 0.0),
                         axis=1, keepdims=True)
            hi, mid, lo = _split3_bf16(ct)
            qs = jnp.where((lane >= 3 * i) & (lane < 3 * i + 3), 1.0, 0.0)
            qs = jnp.where(lane == ones0, hi, qs)
            qs = jnp.where(lane == ones0 + 1, mid, qs)
            qs = jnp.where(lane == ones0 + 2, lo, qs)
            q_heads.append(jnp.concatenate([qa, qs.astype(BF16)], axis=1))
    rr = lax.broadcasted_iota(jnp.int32, (tq, tq), 0)
    cc_i = lax.broadcasted_iota(jnp.int32, (tq, tq), 1)
    causal = rr >= cc_i
    nchunk = tq // LANES

    def chunks(a):
        return [a[:, i * LANES:(i + 1) * LANES] for i in range(nchunk)]

    def attend(step):
        for sub in range(qsub):
            attend_tile(step * qsub + sub, sub)

    def attend_tile(n, sub):
        outs = []
        for idx, (pr, e) in enumerate(heads):
            m = None
            l_part = jnp.zeros((tq, LANES), F32)
            o = jnp.zeros((tq, LANES), F32)
            for j in range(n + 1):
                rows = slice(j * tq, (j + 1) * tq)
                kj = jnp.concatenate([k_ref[0, rows, pr * LANES:(pr + 1) * LANES],
                                      spec_ref[rows, :]], axis=1)
                s = lax.dot_general(q_heads[sub * n_heads + idx], kj, (((1,), (1,)), ((), ())),
                                    preferred_element_type=F32)
                if j == n:
                    s = jnp.where(causal, s, NEG)
                bm = jnp.max(functools.reduce(jnp.maximum, chunks(s)), axis=1, keepdims=True)
                if m is None:
                    m = bm
                else:
                    m_new = jnp.maximum(m, bm)
                    a = jnp.exp2(m - m_new)
                    l_part = l_part * a
                    o = o * a
                    m = m_new
                p = jnp.exp2(s - m)
                l_part = l_part + functools.reduce(lambda a, b: a + b, chunks(p))
                o = o + jnp.dot(p.astype(BF16), v_ref[0, rows, pr * LANES:(pr + 1) * LANES],
                                preferred_element_type=F32)
            outs.append(o / jnp.sum(l_part, axis=1, keepdims=True))
        for pr in range(pps):
            o_ref[0, sub * tq:(sub + 1) * tq, pr * LANES:(pr + 1) * LANES] = jnp.where(
                lo_half, outs[2 * pr], outs[2 * pr + 1]).astype(o_ref.dtype)

    lax.switch(qi, [functools.partial(attend, step) for step in range(nq // qsub)])


FOX_PAIRS_PER_STEP = 4
FOX_TILES_PER_STEP = 2


def _fox_attention(proj, ccol, batch, seq):
    tq = min(256, seq)
    nq = seq // tq
    hd_all = FOX_HEADS * HEAD_DIM
    pps = FOX_PAIRS_PER_STEP
    width = pps * LANES
    groups = hd_all // width
    qsub = min(FOX_TILES_PER_STEP, nq)
    assert nq % qsub == 0
    kern = functools.partial(_fox_kernel, tq=tq, nq=nq, pps=pps)
    return pl.pallas_call(
        kern,
        out_shape=jax.ShapeDtypeStruct((batch, seq, hd_all), BF16),
        grid=(batch, groups, nq // qsub),
        in_specs=[pl.BlockSpec((1, qsub * tq, width), lambda b, h, i: (b, i, h)),
                  pl.BlockSpec((1, seq, width), lambda b, h, i: (b, 0, groups + h)),
                  pl.BlockSpec((1, seq, width), lambda b, h, i: (b, 0, 2 * groups + h)),
                  pl.BlockSpec((1, seq, LANES), lambda b, h, i: (b, 0, 0))],
        out_specs=pl.BlockSpec((1, qsub * tq, width), lambda b, h, i: (b, i, h)),
        scratch_shapes=[pltpu.VMEM((seq, LANES), BF16)],
        compiler_params=_cparams("parallel", "parallel", "arbitrary"),
        name="fox_attention",
    )(proj, proj, proj, ccol)


def _layer_norm(y, gain, bias):
    mu = jnp.mean(y, axis=-1, keepdims=True)
    d = y - mu
    var = jnp.mean(d * d, axis=-1, keepdims=True)
    return d * lax.rsqrt(var + LN_EPS) * gain + bias


def _split_bf16(v):
    hi = v.astype(BF16)
    lo = (v - hi.astype(F32)).astype(BF16)
    return hi, lo


def _route(logits_t):
    rows = [logits_t[e:e + 1, :] for e in range(N_EXPERTS)]
    mx = functools.reduce(jnp.maximum, rows)
    ex = [jnp.exp(v - mx) for v in rows]
    den = functools.reduce(lambda a, b: a + b, ex)
    sc = [v / den for v in ex]
    gscore = []
    for g in range(N_GROUPS):
        mem = sc[g * EXPERTS_PER_GROUP:(g + 1) * EXPERTS_PER_GROUP]
        pairs = [mem[i] + mem[j] for i in range(4) for j in range(i + 1, 4)]
        gscore.append(functools.reduce(jnp.maximum, pairs))
    gmax = functools.reduce(jnp.maximum, gscore)
    taken = None
    gates = []
    for g in range(N_GROUPS):
        eq = gscore[g] == gmax
        best = eq if taken is None else eq & jnp.logical_not(taken)
        taken = eq if taken is None else taken | eq
        mem = sc[g * EXPERTS_PER_GROUP:(g + 1) * EXPERTS_PER_GROUP]
        sel = []
        for i in range(4):
            rank = jnp.zeros_like(mem[i])
            for j in range(4):
                if j == i:
                    continue
                ahead = (mem[j] > mem[i]) | ((mem[j] == mem[i]) & (j < i))
                rank = rank + jnp.where(ahead, 1.0, 0.0)
            sel.append(best & (rank < 2.0))
        tot = functools.reduce(lambda a, b: a + b,
                               [jnp.where(sel[i], mem[i], 0.0) for i in range(4)])
        for i in range(4):
            gates.append(jnp.where(sel[i], mem[i] / tot, 0.0))
    return gates


OUTPROJ_SUBTILES = 2


def _outproj_kernel(a_ref, wo_ref, x_ref, g_ref, b_ref, wr_ref, br_ref, x1_ref, gate_ref):
    sub = x_ref.shape[0] // OUTPROJ_SUBTILES
    wo = wo_ref[...].astype(BF16)
    for h in range(OUTPROJ_SUBTILES):
        rows = slice(h * sub, (h + 1) * sub)
        mix = jnp.dot(a_ref[rows, :], wo, preferred_element_type=F32)
        x1 = _layer_norm(ALPHA * x_ref[rows, :] + mix, g_ref[...], b_ref[...])
        x1_ref[rows, :] = x1
        hi, lo = _split_bf16(x1)
        from_hi = jnp.dot(hi, wr_ref[...], preferred_element_type=F32)
        from_lo = jnp.dot(lo, wr_ref[...], preferred_element_type=F32)
        logits = (from_hi[:, :LANES] + (from_lo[:, :LANES] + from_hi[:, LANES:])) + br_ref[...]
        routed = _route(logits.T[:N_EXPERTS, :])
        pad = jnp.zeros((LANES - len(routed), sub), F32)
        gate_ref[rows, :] = jnp.concatenate(list(routed) + [pad], axis=0).T


def _outproj_ln_router(a2, wo_stack, layer, x2, gain, bias, wr, br):
    t_tokens, d = x2.shape
    tm = min(1024, t_tokens)
    row = lambda i: (i, 0)
    fixed = lambda i: (0, 0)
    return pl.pallas_call(
        _outproj_kernel,
        out_shape=(jax.ShapeDtypeStruct((t_tokens, d), F32),
                   jax.ShapeDtypeStruct((t_tokens, LANES), F32)),
        grid=(t_tokens // tm,),
        in_specs=[pl.BlockSpec((tm, d), row),
                  pl.BlockSpec((None, d, d), lambda i: (layer, 0, 0)),
                  pl.BlockSpec((tm, d), row),
                  pl.BlockSpec((1, d), fixed),
                  pl.BlockSpec((1, d), fixed),
                  pl.BlockSpec((d, 2 * LANES), fixed),
                  pl.BlockSpec((1, LANES), fixed)],
        out_specs=(pl.BlockSpec((tm, d), row), pl.BlockSpec((tm, LANES), row)),
        compiler_params=_cparams("parallel"),
        name="outproj_ln_router",
    )(a2, wo_stack, x2, gain, bias, wr, br)


def _moe_kernel(x_ref, gate_ref, wg_ref, wu_ref, wd_ref, g_ref, b_ref, o_ref, xb_ref, acc_ref):
    grp = pl.program_id(1)

    @pl.when(grp == 0)
    def _():
        xb_ref[...] = x_ref[...].astype(BF16)
        acc_ref[...] = jnp.zeros_like(acc_ref)

    gates = gate_ref[...]
    lane = lax.broadcasted_iota(jnp.int32, gates.shape, 1)
    acts = []
    for k in range(EXPERTS_PER_GROUP):
        xb = xb_ref[...]
        hg = jnp.dot(xb, wg_ref[k].astype(BF16), preferred_element_type=F32)
        hu = jnp.dot(xb, wu_ref[k].astype(BF16), preferred_element_type=F32)
        ge = jnp.sum(jnp.where(lane == grp * EXPERTS_PER_GROUP + k, gates, 0.0),
                     axis=1, keepdims=True)
        acts.append((hg * jax.nn.sigmoid(hg) * hu * ge).astype(BF16))
    wd_all = wd_ref[...].astype(BF16).reshape(EXPERTS_PER_GROUP * D_EXPERT, wd_ref.shape[2])
    acc_ref[...] += jnp.dot(jnp.concatenate(acts, axis=1), wd_all, preferred_element_type=F32)

    @pl.when(grp == N_GROUPS - 1)
    def _():
        o_ref[...] = _layer_norm(ALPHA * x_ref[...] + acc_ref[...], g_ref[...], b_ref[...])


def _moe_ln(x2, gates, w_gate, w_up, w_down, layer, gain, bias):
    t_tokens, d = x2.shape
    tm = min(1024, t_tokens)
    row = lambda i, e: (i, 0)
    fixed = lambda i, e: (0, 0)
    per_group = lambda i, e: (layer, e, 0, 0)
    return pl.pallas_call(
        _moe_kernel,
        out_shape=jax.ShapeDtypeStruct((t_tokens, d), F32),
        grid=(t_tokens // tm, N_GROUPS),
        in_specs=[pl.BlockSpec((tm, d), row),
                  pl.BlockSpec((tm, LANES), row),
                  pl.BlockSpec((None, EXPERTS_PER_GROUP, d, D_EXPERT), per_group),
                  pl.BlockSpec((None, EXPERTS_PER_GROUP, d, D_EXPERT), per_group),
                  pl.BlockSpec((None, EXPERTS_PER_GROUP, D_EXPERT, d), per_group),
                  pl.BlockSpec((1, d), fixed),
                  pl.BlockSpec((1, d), fixed)],
        out_specs=pl.BlockSpec((tm, d), row),
        scratch_shapes=[pltpu.VMEM((tm, d), BF16), pltpu.VMEM((tm, d), F32)],
        compiler_params=_cparams("parallel", "arbitrary"),
        name="moe_ln",
    )(x2, gates, w_gate, w_up, w_down, gain, bias)


def _pad_lanes(w):
    return jnp.pad(w, ((0, 0), (0, LANES - w.shape[1])))


def kernel(x, ln_gain, ln_bias, swa_w_in, swa_sinks, swa_w_o, fox_w_in, fox_b_f, fox_w_o,
           w_router, b_router, w_gate, w_up, w_down):
    batch, seq, d = x.shape
    t_tokens = batch * seq
    tables = _rope_tables(seq)
    wr = jnp.concatenate(_split_bf16(_pad_lanes(w_router)), axis=1)
    br = _pad_lanes(b_router[None, :])
    fox_qkv = 3 * FOX_HEADS * HEAD_DIM
    x2 = x.reshape(t_tokens, d)
    for i in range(DEPTH):
        j = i // 2
        if i % 2 == 0:
            proj = _in_projection(x2, swa_w_in, j, swa_w_in.shape[2], seq,
                                  q_cols=SWA_Q_HEADS * HEAD_DIM, q_scale=QK_SCALE * LOG2E,
                                  rope_cols=(SWA_Q_HEADS + SWA_KV_HEADS) * HEAD_DIM,
                                  tables=tables)
            attn = _swa_attention(proj.reshape(batch, seq, -1), swa_sinks[j] * LOG2E, batch, seq)
            w_o = swa_w_o
        else:
            proj, ccol = _in_projection(x2, fox_w_in, j, fox_qkv, seq,
                                        q_cols=FOX_HEADS * HEAD_DIM, q_scale=QK_SCALE * LOG2E,
                                        rope_cols=0, tables=tables,
                                        decay_bias=_pad_lanes(fox_b_f[j][None, :]))
            attn = _fox_attention(proj.reshape(batch, seq, -1),
                                  ccol.reshape(batch, seq, LANES), batch, seq)
            w_o = fox_w_o
        x1, gates = _outproj_ln_router(attn.reshape(t_tokens, d), w_o, j, x2,
                                       ln_gain[i, 0][None, :], ln_bias[i, 0][None, :],
                                       wr, br)
        x2 = _moe_ln(x1, gates, w_gate, w_up, w_down, i,
                     ln_gain[i, 1][None, :], ln_bias[i, 1][None, :])
    return x2.reshape(batch, seq, d)
```
